```python
import jax, jax.numpy as jnp
from jax import lax
import numpy as np

D_MODEL = 2048
BATCH = 2
SEQ = 8192
DEPTH = 1

GRID_W = 64
CTX_LEN = 256
EPS = 1e-6
N_MOD = 6

M_HEADS = 4
M_HEAD_DIM = 256
M_WIDTH = M_HEADS * M_HEAD_DIM
M_CHUNK = 64
M_GATES = 4 * M_HEADS
M_COLS = 4 * M_WIDTH + M_GATES

CV_WIDTH = D_MODEL - M_WIDTH
CV_KSIZE = 31
CV_COLS = 2 * CV_WIDTH

D_MIX = M_WIDTH + CV_WIDTH
N_COLS = M_COLS + CV_COLS

PEER_HEADS = 8
PEER_NKEYS = 128
PEER_EXPERTS = PEER_NKEYS * PEER_NKEYS
PEER_QDIM = 256
PEER_HALF = PEER_QDIM // 2
PEER_TOPK = 16
PEER_BLOCK = 128

kernel_name = 'hybrid_mlstm_conformer_peer_dit'


def rmsnorm(x, g):
    xf = x.astype(jnp.float32)
    y = xf * lax.rsqrt(jnp.mean(xf * xf, -1, keepdims=True) + EPS)
    return (y * g.astype(jnp.float32)).astype(x.dtype)


def layernorm(x, g, b):
    xf = x.astype(jnp.float32)
    mu = jnp.mean(xf, -1, keepdims=True)
    d = xf - mu
    y = d * lax.rsqrt(jnp.mean(d * d, -1, keepdims=True) + EPS)
    return (y * g.astype(jnp.float32) + b.astype(jnp.float32)).astype(x.dtype)


def modulate(h, shift, scale):
    return h * (1 + scale) + shift


def mlstm_scan(q, k, v, i_pre, f_pre):
    B, H, L, d = q.shape
    nc = L // M_CHUNK
    log_f = jax.nn.log_sigmoid(f_pre)

    def chunks(t):
        t = t.reshape(B, H, nc, M_CHUNK, *t.shape[3:])
        return jnp.moveaxis(t, 2, 0)

    xs = tuple(chunks(t) for t in (q, k, v, i_pre, log_f))
    causal = jnp.tril(jnp.ones((M_CHUNK, M_CHUNK), dtype=bool))

    def step(carry, inp):
        C, n, m = carry
        qb, kb, vb, ib, fb = inp
        b = jnp.cumsum(fb, -1)
        log_d = jnp.where(causal, b[..., :, None] - b[..., None, :] + ib[..., None, :], -jnp.inf)
        inter = b + m[..., None]
        m_row = jnp.maximum(inter, jnp.max(log_d, -1))
        w = jnp.exp(log_d - m_row[..., None])
        s_inter = jnp.exp(inter - m_row)
        qk = jnp.einsum('bhtd,bhsd->bhts', qb, kb) * w
        num = jnp.einsum('bhts,bhse->bhte', qk, vb) + s_inter[..., None] * jnp.einsum('bhtd,bhde->bhte', qb, C)
        den = jnp.sum(qk, -1) + s_inter * jnp.einsum('bhtd,bhd->bht', qb, n)
        h = num / jnp.maximum(jnp.abs(den), jnp.exp(-m_row))[..., None]
        b_last = b[..., -1]
        log_u = b_last[..., None] - b + ib
        m_new = jnp.maximum(b_last + m, jnp.max(log_u, -1))
        a = jnp.exp(b_last + m - m_new)
        u = jnp.exp(log_u - m_new[..., None])
        C_new = a[..., None, None] * C + jnp.einsum('bhs,bhsd,bhse->bhde', u, kb, vb)
        n_new = a[..., None] * n + jnp.einsum('bhs,bhsd->bhd', u, kb)
        return (C_new, n_new, m_new), h

    init = (jnp.zeros((B, H, d, d), jnp.float32), jnp.zeros((B, H, d), jnp.float32), jnp.zeros((B, H), jnp.float32))
    _, h = lax.scan(step, init, xs)
    return jnp.moveaxis(h, 0, 2).reshape(B, H, L, d)


def mlstm_bidir(pc, px, gate_bias, norm_w):
    p = jnp.concatenate([pc, px], 1).astype(jnp.float32)
    B, L, _ = p.shape
    q, k, v, o = [p[..., j * M_WIDTH:(j + 1) * M_WIDTH] for j in range(4)]
    gates = p[..., 4 * M_WIDTH:] + gate_bias.astype(jnp.float32)

    def heads(t):
        return t.reshape(B, L, M_HEADS, M_HEAD_DIM).transpose(0, 2, 1, 3)

    q = heads(q) * (M_HEAD_DIM ** -0.5)
    k = heads(k)
    v = heads(v)
    g = gates.reshape(B, L, 4, M_HEADS).transpose(2, 0, 3, 1)

    def rev(t):
        return jnp.concatenate([jnp.flip(t[:, :, :CTX_LEN], 2), jnp.flip(t[:, :, CTX_LEN:], 2)], 2)

    h_fwd = mlstm_scan(q, k, v, g[0], g[1])
    h_bwd = rev(mlstm_scan(rev(q), rev(k), rev(v), rev(g[2]), rev(g[3])))
    h = h_fwd + h_bwd
    h = h * lax.rsqrt(jnp.mean(h * h, -1, keepdims=True) + EPS)
    h = h.transpose(0, 2, 1, 3).reshape(B, L, M_WIDTH) * norm_w.astype(jnp.float32) * jax.nn.sigmoid(o)
    h = h.astype(pc.dtype)
    return h[:, :CTX_LEN], h[:, CTX_LEN:]


def conformer_conv(p, n_seq, seq_len, conv_w, conv_b, ln_g, ln_b):
    B, N, _ = p.shape
    u = p[..., :CV_WIDTH] * jax.nn.sigmoid(p[..., CV_WIDTH:])
    u = u.reshape(n_seq, seq_len, CV_WIDTH)
    y = lax.conv_general_dilated(u, conv_w[:, None, :], window_strides=(1,),
                                 padding=[(CV_KSIZE // 2, CV_KSIZE // 2)],
                                 dimension_numbers=('NWC', 'WIO', 'NWC'),
                                 feature_group_count=CV_WIDTH) + conv_b
    y = jax.nn.silu(layernorm(y, ln_g, ln_b))
    return y.reshape(B, N, CV_WIDTH)


def peer(h, w_query, sub_keys, u_emb, v_emb):
    B, N, D = h.shape
    t = h.reshape(B * N, D)
    T = t.shape[0]
    q = (t @ w_query).reshape(T, PEER_HEADS, 2, PEER_HALF)
    s = jnp.einsum('thpk,pnk->thpn', q, sub_keys).astype(jnp.float32)
    s_val, s_idx = lax.top_k(s, PEER_TOPK)
    cand = s_val[..., 0, :, None] + s_val[..., 1, None, :]
    cand_idx = s_idx[..., 0, :, None] * PEER_NKEYS + s_idx[..., 1, None, :]
    best, pos = lax.top_k(cand.reshape(T, PEER_HEADS, PEER_TOPK * PEER_TOPK), PEER_TOPK)
    expert = jnp.take_along_axis(cand_idx.reshape(T, PEER_HEADS, PEER_TOPK * PEER_TOPK), pos, -1)
    gate = jax.nn.softmax(best, -1).astype(t.dtype)
    nb = T // PEER_BLOCK

    def block(args):
        tb, eb, gb = args
        ub = jnp.take(u_emb, eb, axis=0)
        vb = jnp.take(v_emb, eb, axis=0)
        act = jax.nn.gelu(jnp.einsum('td,ted->te', tb, ub), approximate=False) * gb
        return jnp.einsum('te,ted->td', act, vb)

    out = lax.map(block, (t.reshape(nb, PEER_BLOCK, D),
                          expert.reshape(nb, PEER_BLOCK, PEER_HEADS * PEER_TOPK),
                          gate.reshape(nb, PEER_BLOCK, PEER_HEADS * PEER_TOPK)))
    return out.reshape(B, N, D)


def setup_inputs(seed: int = 0) -> dict:
    key = jax.random.key(seed)
    ks = jax.random.split(key, 24)
    D = D_MODEL

    def nrm(k, shape, s):
        return jax.random.normal(k, shape, jnp.float32) * s

    gate_base = jnp.concatenate([jnp.zeros((M_HEADS,), jnp.float32), jnp.linspace(3.0, 6.0, M_HEADS, dtype=jnp.float32),
                                 jnp.zeros((M_HEADS,), jnp.float32), jnp.linspace(3.0, 6.0, M_HEADS, dtype=jnp.float32)])
    return {
        'x': nrm(ks[0], (BATCH, SEQ, D), 1.0),
        'c': nrm(ks[1], (BATCH, D), 1.0),
        'ctx': nrm(ks[2], (BATCH, CTX_LEN, D), 1.0),
        'c_ctx': nrm(ks[3], (D,), 1.0),
        'w_mod': nrm(ks[4], (DEPTH, D, N_MOD * D), 0.5 * D ** -0.5),
        'b_mod': nrm(ks[5], (DEPTH, N_MOD * D), 0.02),
        'g_mix': 1.0 + nrm(ks[6], (DEPTH, D), 0.02),
        'g_ffn': 1.0 + nrm(ks[7], (DEPTH, D), 0.02),
        'w_in': nrm(ks[8], (DEPTH, D, N_COLS), D ** -0.5),
        'b_gate': gate_base + nrm(ks[9], (DEPTH, M_GATES), 0.1),
        'g_mhn': 1.0 + nrm(ks[10], (DEPTH, M_WIDTH), 0.02),
        'conv_w': nrm(ks[11], (DEPTH, CV_KSIZE, CV_WIDTH), CV_KSIZE ** -0.5),
        'conv_b': nrm(ks[12], (DEPTH, CV_WIDTH), 0.02),
        'cn_g': 1.0 + nrm(ks[13], (DEPTH, CV_WIDTH), 0.02),
        'cn_b': nrm(ks[14], (DEPTH, CV_WIDTH), 0.02),
        'w_out': nrm(ks[15], (DEPTH, D_MIX, D), D_MIX ** -0.5),
        'w_query': nrm(ks[16], (DEPTH, D, PEER_HEADS * PEER_QDIM), D ** -0.5),
        'sub_keys': nrm(ks[17], (DEPTH, 2, PEER_NKEYS, PEER_HALF), PEER_HALF ** -0.5),
        'u_emb': nrm(ks[18], (DEPTH, PEER_EXPERTS, D), D ** -0.5),
        'v_emb': nrm(ks[19], (DEPTH, PEER_EXPERTS, D), 1.0),
        'g_final': 1.0 + nrm(ks[20], (D,), 0.02),
    }


def reference(x, c, ctx, c_ctx, w_mod, b_mod, g_mix, g_ffn, w_in, b_gate, g_mhn, conv_w, conv_b,
              cn_g, cn_b, w_out, w_query, sub_keys, u_emb, v_emb, g_final):
    B, N, D = x.shape
    rows = N // GRID_W
    for l in range(DEPTH):
        need_ctx = l < DEPTH - 1
        mod_x = (jax.nn.silu(c) @ w_mod[l] + b_mod[l]).reshape(B, N_MOD, 1, D)
        mod_c = (jax.nn.silu(c_ctx)[None] @ w_mod[l] + b_mod[l]).reshape(1, N_MOD, 1, D)
        sh1, sc1, ga1, sh2, sc2, ga2 = [mod_x[:, j] for j in range(N_MOD)]
        csh1, csc1, cga1, csh2, csc2, cga2 = [mod_c[:, j] for j in range(N_MOD)]

        hx = modulate(rmsnorm(x, g_mix[l]), sh1, sc1)
        hc = modulate(rmsnorm(ctx, g_mix[l]), csh1, csc1)
        px = hx @ w_in[l]
        pc_m = hc @ w_in[l, :, :M_COLS]
        m_ctx, m_lat = mlstm_bidir(pc_m, px[..., :M_COLS], b_gate[l], g_mhn[l])
        cv_lat = conformer_conv(px[..., M_COLS:], B * rows, GRID_W, conv_w[l], conv_b[l], cn_g[l], cn_b[l])
        x = x + ga1 * (jnp.concatenate([m_lat, cv_lat], -1) @ w_out[l])

        x = x + ga2 * peer(modulate(rmsnorm(x, g_ffn[l]), sh2, sc2), w_query[l], sub_keys[l], u_emb[l], v_emb[l])

        if need_ctx:
            cv_ctx = conformer_conv(hc @ w_in[l, :, M_COLS:], B, CTX_LEN, conv_w[l], conv_b[l], cn_g[l], cn_b[l])
            ctx = ctx + cga1 * (jnp.concatenate([m_ctx, cv_ctx], -1) @ w_out[l])
            ctx = ctx + cga2 * peer(modulate(rmsnorm(ctx, g_ffn[l]), csh2, csc2), w_query[l], sub_keys[l], u_emb[l], v_emb[l])
    return rmsnorm(x, g_final)
```

```python
import functools

import jax
import jax.numpy as jnp
from jax import lax
from jax.experimental import pallas as pl
from jax.experimental.pallas import tpu as pltpu

GRID_W = 64
EPS = 1e-6
M_HEADS = 4
PEER_HEADS = 8
PEER_TOPK = 16
N_MOD = 6

LANES = 128
SUBLANES = 8
M_CHUNK = 256
ROW_BLOCK = 256
PEER_TOKENS = 64
VMEM_LIMIT = 56 * 1024 * 1024

F32 = jnp.float32
BF16 = jnp.bfloat16
NEG_INF = float("-inf")


def _cparams(*sem):
    return pltpu.CompilerParams(dimension_semantics=sem, vmem_limit_bytes=VMEM_LIMIT)


def _resident(shape):
    nd = len(shape)
    return pl.BlockSpec(shape, lambda *_: (0,) * nd, pipeline_mode=pl.Buffered(1))


def _rms(x, g):
    return x * lax.rsqrt(jnp.mean(x * x, axis=-1, keepdims=True) + EPS) * g


def _dot(a, b):
    return jnp.dot(a, b, preferred_element_type=F32)


def _dot_nt(a, b):
    return lax.dot_general(a, b, (((1,), (1,)), ((), ())), preferred_element_type=F32)


def _mod_kernel(c_ref, w_ref, b_ref, o_ref):
    cc = c_ref[...]
    a = (cc * jax.nn.sigmoid(cc)).astype(BF16)
    o_ref[...] = _dot(a, w_ref[...].astype(BF16)) + b_ref[...]


def _modulation(cc, w_mod, b_mod):
    rows, d = cc.shape
    cols = w_mod.shape[1]
    tn = 1024
    return pl.pallas_call(
        _mod_kernel,
        grid=(cols // tn,),
        in_specs=[pl.BlockSpec((rows, d), lambda j: (0, 0)),
                  pl.BlockSpec((d, tn), lambda j: (0, j)),
                  pl.BlockSpec((1, tn), lambda j: (0, j))],
        out_specs=pl.BlockSpec((rows, tn), lambda j: (0, j)),
        out_shape=jax.ShapeDtypeStruct((rows, cols), F32),
        compiler_params=_cparams("arbitrary"),
        name="modulation",
    )(cc, w_mod, b_mod)


def _inproj_kernel(x_ref, ctx_ref, mod_ref, g_ref, w_ref, wkt_ref, wgc_ref, wgt_ref, bcol_ref, brow_ref,
                   qv_ref, kt_ref, o_ref, cv_ref, gcol_ref, grow_ref, *, nbatch, d, mw, cw):
    b = pl.program_id(0)
    i = pl.program_id(1)
    is_ctx = i == 0
    xin = jnp.where(is_ctx, ctx_ref[0], x_ref[0])
    r = jnp.where(is_ctx, nbatch, b)
    shift = mod_ref[pl.ds(r, 1), 0:d]
    scale = mod_ref[pl.ds(r, 1), d:2 * d]
    hx = (_rms(xin, g_ref[...]) * (1.0 + scale) + shift).astype(BF16)
    qv_ref[0] = _dot(hx, w_ref[:, 0:2 * mw]).astype(BF16)
    o_ref[0] = _dot(hx, w_ref[:, 2 * mw:3 * mw])
    cv_ref[0] = _dot(hx, w_ref[:, 3 * mw:3 * mw + 2 * cw])
    kt_ref[0] = _dot_nt(wkt_ref[...], hx).astype(BF16)
    gcol_ref[0] = _dot(hx, wgc_ref[...]) + brow_ref[...]
    grow_ref[0] = _dot_nt(wgt_ref[...], hx) + bcol_ref[...]


def _input_projection(x, ctx, mod, g_mix, w_main, w_kt, w_gc, w_gt, bias_col, bias_row, *, mw, cw):
    nbatch, n, d = x.shape
    ctx_len = ctx.shape[1]
    tm = ROW_BLOCK
    assert ctx_len == tm and n % tm == 0
    length = ctx_len + n
    nblk = length // tm
    ng = w_gt.shape[0]
    kern = functools.partial(_inproj_kernel, nbatch=nbatch, d=d, mw=mw, cw=cw)
    return pl.pallas_call(
        kern,
        grid=(nbatch, nblk),
        in_specs=[pl.BlockSpec((1, tm, d), lambda b, i: (b, jnp.maximum(i - 1, 0), 0)),
                  pl.BlockSpec((1, tm, d), lambda b, i: (b, 0, 0)),
                  _resident(mod.shape), _resident(g_mix.shape), _resident(w_main.shape),
                  _resident(w_kt.shape), _resident(w_gc.shape), _resident(w_gt.shape),
                  _resident(bias_col.shape), _resident(bias_row.shape)],
        out_specs=[pl.BlockSpec((1, tm, 2 * mw), lambda b, i: (b, i, 0)),
                   pl.BlockSpec((1, mw, tm), lambda b, i: (b, 0, i)),
                   pl.BlockSpec((1, tm, mw), lambda b, i: (b, i, 0)),
                   pl.BlockSpec((1, tm, 2 * cw), lambda b, i: (b, i, 0)),
                   pl.BlockSpec((1, tm, LANES), lambda b, i: (b, i, 0)),
                   pl.BlockSpec((1, ng, tm), lambda b, i: (b, 0, i))],
        out_shape=[jax.ShapeDtypeStruct((nbatch, length, 2 * mw), BF16),
                   jax.ShapeDtypeStruct((nbatch, mw, length), BF16),
                   jax.ShapeDtypeStruct((nbatch, length, mw), F32),
                   jax.ShapeDtypeStruct((nbatch, length, 2 * cw), F32),
                   jax.ShapeDtypeStruct((nbatch, length, LANES), F32),
                   jax.ShapeDtypeStruct((nbatch, ng, length), F32)],
        compiler_params=_cparams("arbitrary", "arbitrary"),
        name="input_projection",
    )(x, ctx, mod, g_mix, w_main, w_kt, w_gc, w_gt, bias_col, bias_row)


def _split3(a):
    a1 = a.astype(BF16)
    r1 = a - a1.astype(F32)
    a2 = r1.astype(BF16)
    a3 = (r1 - a2.astype(F32)).astype(BF16)
    return a1, a2, a3


def _log_sigmoid(g):
    return jnp.minimum(g, 0.0) - jnp.log1p(jnp.exp(-jnp.abs(g)))


def _mlstm_kernel(qv_ref, kt_ref, gcol_ref, grow_ref, h_ref, c_ref, m_ref, *, rev, nh, hd):
    ch = qv_ref.shape[1]

    @pl.when(pl.program_id(1) == 0)
    def _():
        c_ref[...] = jnp.zeros_like(c_ref)
        m_ref[...] = jnp.zeros_like(m_ref)

    row = lax.broadcasted_iota(jnp.int32, (ch, ch), 0)
    col = lax.broadcasted_iota(jnp.int32, (ch, ch), 1)
    allowed = (row <= col) if rev else (row >= col)
    allowed_t = (row >= col) if rev else (row <= col)
    tri = allowed.astype(BF16)
    tri_t = allowed_t.astype(BF16)

    gcol = gcol_ref[0]
    grow = grow_ref[0]
    bcol = sum(_dot(tri, p) for p in _split3(_log_sigmoid(gcol)))
    brow = sum(_dot(p, tri_t) for p in _split3(_log_sigmoid(grow)))
    last = 0 if rev else ch - 1
    ones = jnp.ones((ch, LANES), BF16)

    for h in range(nh):
        gi = (2 * nh if rev else 0) + h
        gf = gi + nh
        q = qv_ref[0, :, h * hd:(h + 1) * hd]
        v = qv_ref[0, :, (nh + h) * hd:(nh + h + 1) * hd]
        kt = kt_ref[0, h * hd:(h + 1) * hd, :]
        b_c = bcol[:, gf:gf + 1]
        b_r = brow[gf:gf + 1, :]
        i_c = gcol[:, gi:gi + 1]
        i_r = grow[gi:gi + 1, :]
        m = m_ref[h, 0:1, 0:1]

        log_d = jnp.where(allowed, b_c - b_r + i_r, NEG_INF)
        inter = b_c + m
        m_row = jnp.maximum(inter, jnp.max(log_d, axis=1, keepdims=True))
        w = jnp.exp(log_d - m_row)
        s_inter = jnp.exp(inter - m_row)
        qk = (_dot(q, kt) * w).astype(BF16)
        v_ext = jnp.concatenate([v, ones], axis=1)
        state = c_ref[h]
        nd = _dot(qk, v_ext) + s_inter * _dot(q, state.astype(BF16))
        den = nd[:, hd:hd + 1]
        h_ref[0, :, h * hd:(h + 1) * hd] = nd[:, 0:hd] / jnp.maximum(jnp.abs(den), jnp.exp(-m_row))

        b_last = brow[gf:gf + 1, last:last + 1]
        m_new = jnp.maximum(b_last + m, jnp.max(b_last - b_r + i_r, axis=1, keepdims=True))
        decay = jnp.exp(b_last + m - m_new)
        u_c = jnp.exp(b_last - b_c + i_c - m_new)
        uv = (u_c * v_ext.astype(F32)).astype(BF16)
        c_ref[h] = decay * state + _dot(kt, uv)
        m_ref[h] = jnp.broadcast_to(m_new, m_ref.shape[1:])


def _mlstm_scan(qv, kt, gcol, grow, *, rev, nh, hd):
    nbatch, length, _ = qv.shape
    ch = M_CHUNK
    nc = length // ch
    mw = nh * hd
    ng = grow.shape[1]
    if rev:
        blk = lambda c: jnp.where(c == 0, 0, nc - c)
    else:
        blk = lambda c: c
    kern = functools.partial(_mlstm_kernel, rev=rev, nh=nh, hd=hd)
    return pl.pallas_call(
        kern,
        grid=(nbatch, nc),
        in_specs=[pl.BlockSpec((1, ch, 2 * mw), lambda b, c: (b, blk(c), 0)),
                  pl.BlockSpec((1, mw, ch), lambda b, c: (b, 0, blk(c))),
                  pl.BlockSpec((1, ch, LANES), lambda b, c: (b, blk(c), 0)),
                  pl.BlockSpec((1, ng, ch), lambda b, c: (b, 0, blk(c)))],
        out_specs=pl.BlockSpec((1, ch, mw), lambda b, c: (b, blk(c), 0)),
        out_shape=jax.ShapeDtypeStruct((nbatch, length, mw), F32),
        scratch_shapes=[pltpu.VMEM((nh, hd, hd + LANES), F32),
                        pltpu.VMEM((nh, SUBLANES, LANES), F32)],
        compiler_params=_cparams("arbitrary", "arbitrary"),
        name="mlstm_bwd" if rev else "mlstm_fwd",
    )(qv, kt, gcol, grow)


def _conv_kernel(p_ref, w_ref, b_ref, g_ref, beta_ref, o_ref, pad_ref, y_ref, *, ks, seq, cw):
    tb = p_ref.shape[1]
    nseq = tb // seq
    half = ks // 2
    pad = 2 * SUBLANES
    assert half <= pad
    nchunk = cw // LANES

    u = p_ref[0, :, 0:cw] * jax.nn.sigmoid(p_ref[0, :, cw:2 * cw])
    zeros = jnp.zeros((pad, cw), F32)
    for s in range(nseq):
        pad_ref[s, 0:pad, :] = zeros
        pad_ref[s, pad:pad + seq, :] = u[s * seq:(s + 1) * seq]
        pad_ref[s, pad + seq:pad + seq + pad, :] = zeros

    def body(idx, carry):
        s = idx // nchunk
        c0 = pl.multiple_of((idx % nchunk) * LANES, LANES)
        acc = jnp.zeros((seq, LANES), F32)
        for k in range(ks):
            acc = acc + pad_ref[s, pl.ds(pad - half + k, seq), pl.ds(c0, LANES)] * w_ref[k:k + 1, pl.ds(c0, LANES)]
        y_ref[pl.ds(pl.multiple_of(s * seq, seq), seq), pl.ds(c0, LANES)] = acc + b_ref[:, pl.ds(c0, LANES)]
        return carry

    lax.fori_loop(0, nseq * nchunk, body, 0)

    y = y_ref[...]
    mu = jnp.mean(y, axis=-1, keepdims=True)
    dev = y - mu
    z = dev * lax.rsqrt(jnp.mean(dev * dev, axis=-1, keepdims=True) + EPS) * g_ref[...] + beta_ref[...]
    o_ref[0] = (z * jax.nn.sigmoid(z)).astype(BF16)


def _conformer_conv(cv, conv_w, conv_b, ln_g, ln_b, *, ctx_len, seq):
    nbatch, length, cw2 = cv.shape
    cw = cw2 // 2
    n = length - ctx_len
    tb = ROW_BLOCK
    assert ctx_len % tb == 0 and n % tb == 0 and tb % seq == 0
    off = ctx_len // tb
    ks = conv_w.shape[0]
    kern = functools.partial(_conv_kernel, ks=ks, seq=seq, cw=cw)
    return pl.pallas_call(
        kern,
        grid=(nbatch, n // tb),
        in_specs=[pl.BlockSpec((1, tb, cw2), lambda b, i: (b, i + off, 0)),
                  _resident(conv_w.shape), _resident(conv_b.shape), _resident(ln_g.shape), _resident(ln_b.shape)],
        out_specs=pl.BlockSpec((1, tb, cw), lambda b, i: (b, i, 0)),
        out_shape=jax.ShapeDtypeStruct((nbatch, n, cw), BF16),
        scratch_shapes=[pltpu.VMEM((tb // seq, seq + 4 * SUBLANES, cw), F32),
                        pltpu.VMEM((tb, cw), F32)],
        compiler_params=_cparams("arbitrary", "arbitrary"),
        name="conformer_conv",
    )(cv, conv_w, conv_b, ln_g, ln_b)


def _outproj_kernel(hf_ref, hb_ref, o_ref, cv_ref, x_ref, mod_ref, gm_ref, wo_ref, gf_ref, wq_ref,
                    x1_ref, t2_ref, qp_ref, *, d, nh, hd):
    b = pl.program_id(0)
    hsum = hf_ref[0] + hb_ref[0]
    parts = []
    for h in range(nh):
        seg = hsum[:, h * hd:(h + 1) * hd]
        parts.append(seg * lax.rsqrt(jnp.mean(seg * seg, axis=-1, keepdims=True) + EPS))
    hn = jnp.concatenate(parts, axis=1)
    m_lat = (hn * gm_ref[...] * jax.nn.sigmoid(o_ref[0])).astype(BF16)
    lhs = jnp.concatenate([m_lat, cv_ref[0]], axis=1)
    gate1 = mod_ref[pl.ds(b, 1), 2 * d:3 * d]
    x1 = x_ref[0] + gate1 * _dot(lhs, wo_ref[...])
    x1_ref[0] = x1
    shift2 = mod_ref[pl.ds(b, 1), 3 * d:4 * d]
    scale2 = mod_ref[pl.ds(b, 1), 4 * d:5 * d]
    t2 = _rms(x1, gf_ref[...]) * (1.0 + scale2) + shift2
    t2_ref[0] = t2
    qp_ref[0] = _dot(t2.astype(BF16), wq_ref[...]).astype(BF16)


def _output_projection(hf, hb, o, cv_lat, x, mod, g_mhn, w_out, g_ffn, w_query, *, ctx_len, nh, hd):
    nbatch, n, d = x.shape
    mw = nh * hd
    cw = cv_lat.shape[2]
    tm = ROW_BLOCK
    off = ctx_len // tm
    qd = w_query.shape[1]
    kern = functools.partial(_outproj_kernel, d=d, nh=nh, hd=hd)
    lat = lambda b, i: (b, i + off, 0)
    cur = lambda b, i: (b, i, 0)
    return pl.pallas_call(
        kern,
        grid=(nbatch, n // tm),
        in_specs=[pl.BlockSpec((1, tm, mw), lat), pl.BlockSpec((1, tm, mw), lat), pl.BlockSpec((1, tm, mw), lat),
                  pl.BlockSpec((1, tm, cw), cur), pl.BlockSpec((1, tm, d), cur),
                  _resident(mod.shape), _resident(g_mhn.shape), _resident(w_out.shape),
                  _resident(g_ffn.shape), _resident(w_query.shape)],
        out_specs=[pl.BlockSpec((1, tm, d), cur), pl.BlockSpec((1, tm, d), cur), pl.BlockSpec((1, tm, qd), cur)],
        out_shape=[jax.ShapeDtypeStruct((nbatch, n, d), F32),
                   jax.ShapeDtypeStruct((nbatch, n, d), F32),
                   jax.ShapeDtypeStruct((nbatch, n, qd), BF16)],
        compiler_params=_cparams("arbitrary", "arbitrary"),
        name="output_projection",
    )(hf, hb, o, cv_lat, x, mod, g_mhn, w_out, g_ffn, w_query)


def _top_rows(s_ref, pay_ref, val_ref, sel_ref, k):
    n = s_ref.shape[0]
    iota = lax.broadcasted_iota(jnp.int32, s_ref.shape, 0)

    def step(r, carry):
        s = s_ref[...]
        m = jnp.max(s, axis=0, keepdims=True)
        first = jnp.min(jnp.where(s == m, iota, n), axis=0, keepdims=True)
        hit = iota == first
        val_ref[pl.ds(r, 1), :] = m
        if pay_ref is None:
            sel_ref[pl.ds(r, 1), :] = first
        else:
            sel_ref[pl.ds(r, 1), :] = jnp.max(jnp.where(hit, pay_ref[...], -1), axis=0, keepdims=True)
        s_ref[...] = jnp.where(hit, NEG_INF, s)
        return carry

    lax.fori_loop(0, k, step, 0)


def _route_kernel(q_ref, keys_ref, e_ref, g_ref, s_ref, cand_ref, cidx_ref, va_ref, ia_ref, vb_ref, ib_ref,
                  best_ref, ex_ref, *, nheads, nkeys, half, topk):
    def head(h, carry):
        base = pl.multiple_of(h * 2 * half, 2 * half)
        s_ref[...] = _dot_nt(keys_ref[0], q_ref[:, pl.ds(base, half)])
        _top_rows(s_ref, None, va_ref, ia_ref, topk)
        s_ref[...] = _dot_nt(keys_ref[1], q_ref[:, pl.ds(base + half, half)])
        _top_rows(s_ref, None, vb_ref, ib_ref, topk)
        vb = vb_ref[...]
        ib = ib_ref[...]
        for a in range(topk):
            cand_ref[a * topk:(a + 1) * topk, :] = va_ref[a:a + 1, :] + vb
            cidx_ref[a * topk:(a + 1) * topk, :] = ia_ref[a:a + 1, :] * nkeys + ib
        _top_rows(cand_ref, cidx_ref, best_ref, ex_ref, topk)
        best = best_ref[...]
        ex = jnp.exp(best - best[0:1])
        rows = pl.ds(pl.multiple_of(h * topk, topk), topk)
        e_ref[rows, :] = ex_ref[...]
        g_ref[rows, :] = ex / jnp.sum(ex, axis=0, keepdims=True)
        return carry

    lax.fori_loop(0, nheads, head, 0)


def _peer_route(qp, keys, *, nheads, topk):
    t, qd = qp.shape
    nkeys, half = keys.shape[1], keys.shape[2]
    tb = 512 if t % 512 == 0 else ROW_BLOCK
    kern = functools.partial(_route_kernel, nheads=nheads, nkeys=nkeys, half=half, topk=topk)
    ne = nheads * topk
    return pl.pallas_call(
        kern,
        grid=(t // tb,),
        in_specs=[pl.BlockSpec((tb, qd), lambda i: (i, 0)), _resident(keys.shape)],
        out_specs=[pl.BlockSpec((ne, tb), lambda i: (0, i)), pl.BlockSpec((ne, tb), lambda i: (0, i))],
        out_shape=[jax.ShapeDtypeStruct((ne, t), jnp.int32), jax.ShapeDtypeStruct((ne, t), F32)],
        scratch_shapes=[pltpu.VMEM((nkeys, tb), F32),
                        pltpu.VMEM((topk * topk, tb), F32), pltpu.VMEM((topk * topk, tb), jnp.int32),
                        pltpu.VMEM((topk, tb), F32), pltpu.VMEM((topk, tb), jnp.int32),
                        pltpu.VMEM((topk, tb), F32), pltpu.VMEM((topk, tb), jnp.int32),
                        pltpu.VMEM((topk, tb), F32), pltpu.VMEM((topk, tb), jnp.int32)],
        compiler_params=_cparams("arbitrary"),
        name="peer_route",
    )(qp, keys)


def _pack_kernel(uv_ref, o_ref):
    o_ref[...] = pltpu.bitcast(uv_ref[...], jnp.uint32)


def _pack_tables(u_emb, v_emb):
    e, d = u_emb.shape
    eb = 512
    uv = jnp.stack([u_emb.astype(BF16), v_emb.astype(BF16)], axis=1).reshape(2 * e, d)
    return pl.pallas_call(
        _pack_kernel,
        grid=(e // eb,),
        in_specs=[pl.BlockSpec((2 * eb, d), lambda i: (i, 0))],
        out_specs=pl.BlockSpec((eb, d), lambda i: (i, 0)),
        out_shape=jax.ShapeDtypeStruct((e, d), jnp.uint32),
        compiler_params=_cparams("arbitrary"),
        name="pack_tables",
    )(uv)


def _gelu(x):
    return 0.5 * x * (1.0 + lax.erf(x * 0.7071067811865476))


def _peer_kernel(ids_ref, gate_ref, t_ref, x1_ref, mod_ref, gfin_ref, tab_ref, o_ref, buf_ref, acc_ref, sem,
                 *, d, n_per_batch):
    tb, ne = ids_ref.shape
    b = (pl.program_id(0) * tb) // n_per_batch

    def issue(j, slot):
        for r in range(ne):
            pltpu.make_async_copy(tab_ref.at[pl.ds(ids_ref[j, r], 1), :],
                                  buf_ref.at[slot, pl.ds(r, 1), :], sem.at[slot]).start()

    def wait(slot):
        pltpu.make_async_copy(tab_ref.at[pl.ds(0, ne), :], buf_ref.at[slot], sem.at[slot]).wait()

    issue(0, 0)

    def body(j, carry):
        slot = j % 2

        @pl.when(j + 1 < tb)
        def _():
            issue(j + 1, 1 - slot)

        wait(slot)
        uv = pltpu.bitcast(buf_ref[slot], BF16)
        tj = jnp.broadcast_to(t_ref[pl.ds(j, 1), :], (SUBLANES, d)).astype(BF16)
        score = _dot_nt(tj, uv)
        act = pltpu.roll(_gelu(score), shift=1, axis=1) * gate_ref[pl.ds(j, 1), :]
        acc_ref[pl.ds(j, 1), :] = _dot(act.astype(BF16), uv)[0:1]
        return carry

    lax.fori_loop(0, tb, body, 0)

    x2 = x1_ref[...] + mod_ref[pl.ds(b, 1), 5 * d:6 * d] * acc_ref[...]
    o_ref[...] = _rms(x2, gfin_ref[...])


def _peer_mix(ids, gate2, t2, x1, mod, g_final, table, *, n_per_batch):
    t, d = t2.shape
    ne = ids.shape[1]
    tb = PEER_TOKENS
    assert n_per_batch % tb == 0
    kern = functools.partial(_peer_kernel, d=d, n_per_batch=n_per_batch)
    row = lambda i: (i, 0)
    return pl.pallas_call(
        kern,
        grid=(t // tb,),
        in_specs=[pl.BlockSpec((tb, ne), row, memory_space=pltpu.SMEM),
                  pl.BlockSpec((tb, 2 * ne), row),
                  pl.BlockSpec((tb, d), row), pl.BlockSpec((tb, d), row),
                  _resident(mod.shape), _resident(g_final.shape),
                  pl.BlockSpec(memory_space=pl.ANY)],
        out_specs=pl.BlockSpec((tb, d), row),
        out_shape=jax.ShapeDtypeStruct((t, d), F32),
        scratch_shapes=[pltpu.VMEM((2, ne, d), jnp.uint32),
                        pltpu.VMEM((tb, d), F32),
                        pltpu.SemaphoreType.DMA((2,))],
        compiler_params=_cparams("arbitrary"),
        name="peer_mix",
    )(ids, gate2, t2, x1, mod, g_final, table)


def kernel(x, c, ctx, c_ctx, w_mod, b_mod, g_mix, g_ffn, w_in, b_gate, g_mhn, conv_w, conv_b,
           cn_g, cn_b, w_out, w_query, sub_keys, u_emb, v_emb, g_final):
    nbatch, n, d = x.shape
    ctx_len = ctx.shape[1]
    assert w_mod.shape[0] == 1, "single-layer kernel"
    mw = g_mhn.shape[1]
    cw = conv_b.shape[1]
    nh = M_HEADS
    hd = mw // nh
    ng = b_gate.shape[1]
    assert ng == 4 * nh and ng <= LANES

    cc = jnp.zeros((SUBLANES, d), F32).at[:nbatch].set(c).at[nbatch].set(c_ctx)
    mod = _modulation(cc, w_mod[0], b_mod)

    w = w_in[0]
    w_q, w_k, w_v, w_o = (w[:, j * mw:(j + 1) * mw] for j in range(4))
    w_g = w[:, 4 * mw:4 * mw + ng]
    w_cv = w[:, 4 * mw + ng:]
    w_main = jnp.concatenate([w_q * (hd ** -0.5), w_v, w_o, w_cv], axis=1).astype(BF16)
    w_kt = w_k.T.astype(BF16)
    w_gc = jnp.pad(w_g, ((0, 0), (0, LANES - ng))).astype(BF16)
    w_gt = w_g.T.astype(BF16)
    bias_row = jnp.pad(b_gate, ((0, 0), (0, LANES - ng)))
    bias_col = b_gate.reshape(ng, 1)

    qv, kt, o, cv, gcol, grow = _input_projection(x, ctx, mod, g_mix, w_main, w_kt, w_gc, w_gt,
                                                  bias_col, bias_row, mw=mw, cw=cw)
    hf = _mlstm_scan(qv, kt, gcol, grow, rev=False, nh=nh, hd=hd)
    hb = _mlstm_scan(qv, kt, gcol, grow, rev=True, nh=nh, hd=hd)
    cv_lat = _conformer_conv(cv, conv_w[0], conv_b, cn_g, cn_b, ctx_len=ctx_len, seq=GRID_W)
    x1, t2, qp = _output_projection(hf, hb, o, cv_lat, x, mod, g_mhn, w_out[0].astype(BF16), g_ffn,
                                    w_query[0].astype(BF16), ctx_len=ctx_len, nh=nh, hd=hd)

    t = nbatch * n
    eidx, gate = _peer_route(qp.reshape(t, -1), sub_keys[0].astype(BF16), nheads=PEER_HEADS, topk=PEER_TOPK)
    ids = eidx.T
    gate2 = jnp.stack([jnp.zeros_like(gate.T), gate.T], axis=-1).reshape(t, -1)
    table = _pack_tables(u_emb[0], v_emb[0])
    out = _peer_mix(ids, gate2, t2.reshape(t, d), x1.reshape(t, d), mod, g_final.reshape(1, d), table,
                    n_per_batch=n)
    return out.reshape(nbatch, n, d)
```

```python
import functools

import jax
import jax.numpy as jnp
from jax import lax
from jax.experimental import pallas as pl
from jax.experimental.pallas import tpu as pltpu

GRID_W = 64
EPS = 1e-6
M_HEADS = 4
PEER_HEADS = 8
PEER_TOPK = 16
N_MOD = 6

LANES = 128
SUBLANES = 8
M_CHUNK = 256
ROW_BLOCK = 256
PEER_TOKENS = 64
VMEM_LIMIT = 56 * 1024 * 1024

F32 = jnp.float32
BF16 = jnp.bfloat16
NEG_INF = float("-inf")


def _cparams(*sem):
    return pltpu.CompilerParams(dimension_semantics=sem, vmem_limit_bytes=VMEM_LIMIT)


def _resident(shape):
    nd = len(shape)
    return pl.BlockSpec(shape, lambda *_: (0,) * nd, pipeline_mode=pl.Buffered(1))


def _rms(x, g):
    return x * lax.rsqrt(jnp.mean(x * x, axis=-1, keepdims=True) + EPS) * g


def _dot(a, b):
    return jnp.dot(a, b, preferred_element_type=F32)


def _dot_nt(a, b):
    return lax.dot_general(a, b, (((1,), (1,)), ((), ())), preferred_element_type=F32)


def _mod_kernel(c_ref, w_ref, b_ref, o_ref):
    cc = c_ref[...]
    a = (cc * jax.nn.sigmoid(cc)).astype(BF16)
    o_ref[...] = _dot(a, w_ref[...].astype(BF16)) + b_ref[...]


def _modulation(cc, w_mod, b_mod):
    rows, d = cc.shape
    cols = w_mod.shape[1]
    tn = 1024
    return pl.pallas_call(
        _mod_kernel,
        grid=(cols // tn,),
        in_specs=[pl.BlockSpec((rows, d), lambda j: (0, 0)),
                  pl.BlockSpec((d, tn), lambda j: (0, j)),
                  pl.BlockSpec((1, tn), lambda j: (0, j))],
        out_specs=pl.BlockSpec((rows, tn), lambda j: (0, j)),
        out_shape=jax.ShapeDtypeStruct((rows, cols), F32),
        compiler_params=_cparams("arbitrary"),
        name="modulation",
    )(cc, w_mod, b_mod)


def _inproj_kernel(x_ref, ctx_ref, mod_ref, g_ref, w_ref, wkt_ref, wgc_ref, wgt_ref, bcol_ref, brow_ref,
                   qv_ref, kt_ref, o_ref, cv_ref, gcol_ref, grow_ref, *, nbatch, d, mw, cw):
    b = pl.program_id(0)
    i = pl.program_id(1)
    is_ctx = i == 0
    xin = jnp.where(is_ctx, ctx_ref[0], x_ref[0])
    r = jnp.where(is_ctx, nbatch, b)
    shift = mod_ref[pl.ds(r, 1), 0:d]
    scale = mod_ref[pl.ds(r, 1), d:2 * d]
    hx = (_rms(xin, g_ref[...]) * (1.0 + scale) + shift).astype(BF16)
    qv_ref[0] = _dot(hx, w_ref[:, 0:2 * mw]).astype(BF16)
    o_ref[0] = _dot(hx, w_ref[:, 2 * mw:3 * mw])
    cv_ref[0] = _dot(hx, w_ref[:, 3 * mw:3 * mw + 2 * cw])
    kt_ref[0] = _dot_nt(wkt_ref[...], hx).astype(BF16)
    gcol_ref[0] = _dot(hx, wgc_ref[...]) + brow_ref[...]
    grow_ref[0] = _dot_nt(wgt_ref[...], hx) + bcol_ref[...]


def _input_projection(x, ctx, mod, g_mix, w_main, w_kt, w_gc, w_gt, bias_col, bias_row, *, mw, cw):
    nbatch, n, d = x.shape
    ctx_len = ctx.shape[1]
    tm = ROW_BLOCK
    assert ctx_len == tm and n % tm == 0
    length = ctx_len + n
    nblk = length // tm
    ng = w_gt.shape[0]
    kern = functools.partial(_inproj_kernel, nbatch=nbatch, d=d, mw=mw, cw=cw)
    return pl.pallas_call(
        kern,
        grid=(nbatch, nblk),
        in_specs=[pl.BlockSpec((1, tm, d), lambda b, i: (b, jnp.maximum(i - 1, 0), 0)),
                  pl.BlockSpec((1, tm, d), lambda b, i: (b, 0, 0)),
                  _resident(mod.shape), _resident(g_mix.shape), _resident(w_main.shape),
                  _resident(w_kt.shape), _resident(w_gc.shape), _resident(w_gt.shape),
                  _resident(bias_col.shape), _resident(bias_row.shape)],
        out_specs=[pl.BlockSpec((1, tm, 2 * mw), lambda b, i: (b, i, 0)),
                   pl.BlockSpec((1, mw, tm), lambda b, i: (b, 0, i)),
                   pl.BlockSpec((1, tm, mw), lambda b, i: (b, i, 0)),
                   pl.BlockSpec((1, tm, 2 * cw), lambda b, i: (b, i, 0)),
                   pl.BlockSpec((1, tm, LANES), lambda b, i: (b, i, 0)),
                   pl.BlockSpec((1, ng, tm), lambda b, i: (b, 0, i))],
        out_shape=[jax.ShapeDtypeStruct((nbatch, length, 2 * mw), BF16),
                   jax.ShapeDtypeStruct((nbatch, mw, length), BF16),
                   jax.ShapeDtypeStruct((nbatch, length, mw), F32),
                   jax.ShapeDtypeStruct((nbatch, length, 2 * cw), F32),
                   jax.ShapeDtypeStruct((nbatch, length, LANES), F32),
                   jax.ShapeDtypeStruct((nbatch, ng, length), F32)],
        compiler_params=_cparams("arbitrary", "arbitrary"),
        name="input_projection",
    )(x, ctx, mod, g_mix, w_main, w_kt, w_gc, w_gt, bias_col, bias_row)


def _split3(a):
    a1 = a.astype(BF16)
    r1 = a - a1.astype(F32)
    a2 = r1.astype(BF16)
    a3 = (r1 - a2.astype(F32)).astype(BF16)
    return a1, a2, a3


def _log_sigmoid(g):
    return jnp.minimum(g, 0.0) - jnp.log1p(jnp.exp(-jnp.abs(g)))


def _mlstm_kernel(qv_ref, kt_ref, gcol_ref, grow_ref, h_ref, c_ref, m_ref, *, rev, nh, hd):
    ch = qv_ref.shape[1]

    @pl.when(pl.program_id(1) == 0)
    def _():
        c_ref[...] = jnp.zeros_like(c_ref)
        m_ref[...] = jnp.zeros_like(m_ref)

    row = lax.broadcasted_iota(jnp.int32, (ch, ch), 0)
    col = lax.broadcasted_iota(jnp.int32, (ch, ch), 1)
    allowed = (row <= col) if rev else (row >= col)
    allowed_t = (row >= col) if rev else (row <= col)
    tri = allowed.astype(BF16)
    tri_t = allowed_t.astype(BF16)

    gcol = gcol_ref[0]
    grow = grow_ref[0]
    bcol = sum(_dot(tri, p) for p in _split3(_log_sigmoid(gcol)))
    brow = sum(_dot(p, tri_t) for p in _split3(_log_sigmoid(grow)))
    last = 0 if rev else ch - 1
    ones = jnp.ones((ch, LANES), BF16)

    for h in range(nh):
        gi = (2 * nh if rev else 0) + h
        gf = gi + nh
        q = qv_ref[0, :, h * hd:(h + 1) * hd]
        v = qv_ref[0, :, (nh + h) * hd:(nh + h + 1) * hd]
        kt = kt_ref[0, h * hd:(h + 1) * hd, :]
        b_c = bcol[:, gf:gf + 1]
        b_r = brow[gf:gf + 1, :]
        i_c = gcol[:, gi:gi + 1]
        i_r = grow[gi:gi + 1, :]
        m = m_ref[h, 0:1, 0:1]

        log_d = jnp.where(allowed, b_c - b_r + i_r, NEG_INF)
        inter = b_c + m
        m_row = jnp.maximum(inter, jnp.max(log_d, axis=1, keepdims=True))
        w = jnp.exp(log_d - m_row)
        s_inter = jnp.exp(inter - m_row)
        qk = (_dot(q, kt) * w).astype(BF16)
        v_ext = jnp.concatenate([v, ones], axis=1)
        state = c_ref[h]
        nd = _dot(qk, v_ext) + s_inter * _dot(q, state.astype(BF16))
        den = nd[:, hd:hd + 1]
        h_ref[0, :, h * hd:(h + 1) * hd] = nd[:, 0:hd] / jnp.maximum(jnp.abs(den), jnp.exp(-m_row))

        b_last = brow[gf:gf + 1, last:last + 1]
        m_new = jnp.maximum(b_last + m, jnp.max(b_last - b_r + i_r, axis=1, keepdims=True))
        decay = jnp.exp(b_last + m - m_new)
        u_c = jnp.exp(b_last - b_c + i_c - m_new)
        uv = (u_c * v_ext.astype(F32)).astype(BF16)
        c_ref[h] = decay * state + _dot(kt, uv)
        m_ref[h] = jnp.broadcast_to(m_new, m_ref.shape[1:])


def _mlstm_scan(qv, kt, gcol, grow, *, rev, nh, hd):
    nbatch, length, _ = qv.shape
    ch = M_CHUNK
    nc = length // ch
    mw = nh * hd
    ng = grow.shape[1]
    if rev:
        blk = lambda c: jnp.where(c == 0, 0, nc - c)
    else:
        blk = lambda c: c
    kern = functools.partial(_mlstm_kernel, rev=rev, nh=nh, hd=hd)
    return pl.pallas_call(
        kern,
        grid=(nbatch, nc),
        in_specs=[pl.BlockSpec((1, ch, 2 * mw), lambda b, c: (b, blk(c), 0)),
                  pl.BlockSpec((1, mw, ch), lambda b, c: (b, 0, blk(c))),
                  pl.BlockSpec((1, ch, LANES), lambda b, c: (b, blk(c), 0)),
                  pl.BlockSpec((1, ng, ch), lambda b, c: (b, 0, blk(c)))],
        out_specs=pl.BlockSpec((1, ch, mw), lambda b, c: (b, blk(c), 0)),
        out_shape=jax.ShapeDtypeStruct((nbatch, length, mw), F32),
        scratch_shapes=[pltpu.VMEM((nh, hd, hd + LANES), F32),
                        pltpu.VMEM((nh, SUBLANES, LANES), F32)],
        compiler_params=_cparams("arbitrary", "arbitrary"),
        name="mlstm_bwd" if rev else "mlstm_fwd",
    )(qv, kt, gcol, grow)


def _conv_kernel(p_ref, w_ref, b_ref, g_ref, beta_ref, o_ref, pad_ref, y_ref, *, ks, seq, cw):
    tb = p_ref.shape[1]
    nseq = tb // seq
    half = ks // 2
    pad = 2 * SUBLANES
    assert half <= pad
    nchunk = cw // LANES

    u = p_ref[0, :, 0:cw] * jax.nn.sigmoid(p_ref[0, :, cw:2 * cw])
    zeros = jnp.zeros((pad, cw), F32)
    for s in range(nseq):
        pad_ref[s, 0:pad, :] = zeros
        pad_ref[s, pad:pad + seq, :] = u[s * seq:(s + 1) * seq]
        pad_ref[s, pad + seq:pad + seq + pad, :] = zeros

    def body(idx, carry):
        s = idx // nchunk
        c0 = pl.multiple_of((idx % nchunk) * LANES, LANES)
        acc = jnp.zeros((seq, LANES), F32)
        for k in range(ks):
            acc = acc + pad_ref[s, pl.ds(pad - half + k, seq), pl.ds(c0, LANES)] * w_ref[k:k + 1, pl.ds(c0, LANES)]
        y_ref[pl.ds(pl.multiple_of(s * seq, seq), seq), pl.ds(c0, LANES)] = acc + b_ref[:, pl.ds(c0, LANES)]
        return carry

    lax.fori_loop(0, nseq * nchunk, body, 0)

    y = y_ref[...]
    mu = jnp.mean(y, axis=-1, keepdims=True)
    dev = y - mu
    z = dev * lax.rsqrt(jnp.mean(dev * dev, axis=-1, keepdims=True) + EPS) * g_ref[...] + beta_ref[...]
    o_ref[0] = (z * jax.nn.sigmoid(z)).astype(BF16)


def _conformer_conv(cv, conv_w, conv_b, ln_g, ln_b, *, ctx_len, seq):
    nbatch, length, cw2 = cv.shape
    cw = cw2 // 2
    n = length - ctx_len
    tb = ROW_BLOCK
    assert ctx_len % tb == 0 and n % tb == 0 and tb % seq == 0
    off = ctx_len // tb
    ks = conv_w.shape[0]
    kern = functools.partial(_conv_kernel, ks=ks, seq=seq, cw=cw)
    return pl.pallas_call(
        kern,
        grid=(nbatch, n // tb),
        in_specs=[pl.BlockSpec((1, tb, cw2), lambda b, i: (b, i + off, 0)),
                  _resident(conv_w.shape), _resident(conv_b.shape), _resident(ln_g.shape), _resident(ln_b.shape)],
        out_specs=pl.BlockSpec((1, tb, cw), lambda b, i: (b, i, 0)),
        out_shape=jax.ShapeDtypeStruct((nbatch, n, cw), BF16),
        scratch_shapes=[pltpu.VMEM((tb // seq, seq + 4 * SUBLANES, cw), F32),
                        pltpu.VMEM((tb, cw), F32)],
        compiler_params=_cparams("arbitrary", "arbitrary"),
        name="conformer_conv",
    )(cv, conv_w, conv_b, ln_g, ln_b)


def _outproj_kernel(hf_ref, hb_ref, o_ref, cv_ref, x_ref, mod_ref, gm_ref, wo_ref, gf_ref, wq_ref,
                    x1_ref, t2_ref, qp_ref, *, d, nh, hd):
    b = pl.program_id(0)
    hsum = hf_ref[0] + hb_ref[0]
    parts = []
    for h in range(nh):
        seg = hsum[:, h * hd:(h + 1) * hd]
        parts.append(seg * lax.rsqrt(jnp.mean(seg * seg, axis=-1, keepdims=True) + EPS))
    hn = jnp.concatenate(parts, axis=1)
    m_lat = (hn * gm_ref[...] * jax.nn.sigmoid(o_ref[0])).astype(BF16)
    lhs = jnp.concatenate([m_lat, cv_ref[0]], axis=1)
    gate1 = mod_ref[pl.ds(b, 1), 2 * d:3 * d]
    x1 = x_ref[0] + gate1 * _dot(lhs, wo_ref[...])
    x1_ref[0] = x1
    shift2 = mod_ref[pl.ds(b, 1), 3 * d:4 * d]
    scale2 = mod_ref[pl.ds(b, 1), 4 * d:5 * d]
    t2 = _rms(x1, gf_ref[...]) * (1.0 + scale2) + shift2
    t2_ref[0] = t2
    qp_ref[0] = _dot(t2.astype(BF16), wq_ref[...]).astype(BF16)


def _output_projection(hf, hb, o, cv_lat, x, mod, g_mhn, w_out, g_ffn, w_query, *, ctx_len, nh, hd):
    nbatch, n, d = x.shape
    mw = nh * hd
    cw = cv_lat.shape[2]
    tm = ROW_BLOCK
    off = ctx_len // tm
    qd = w_query.shape[1]
    kern = functools.partial(_outproj_kernel, d=d, nh=nh, hd=hd)
    lat = lambda b, i: (b, i + off, 0)
    cur = lambda b, i: (b, i, 0)
    return pl.pallas_call(
        kern,
        grid=(nbatch, n // tm),
        in_specs=[pl.BlockSpec((1, tm, mw), lat), pl.BlockSpec((1, tm, mw), lat), pl.BlockSpec((1, tm, mw), lat),
                  pl.BlockSpec((1, tm, cw), cur), pl.BlockSpec((1, tm, d), cur),
                  _resident(mod.shape), _resident(g_mhn.shape), _resident(w_out.shape),
                  _resident(g_ffn.shape), _resident(w_query.shape)],
        out_specs=[pl.BlockSpec((1, tm, d), cur), pl.BlockSpec((1, tm, d), cur), pl.BlockSpec((1, tm, qd), cur)],
        out_shape=[jax.ShapeDtypeStruct((nbatch, n, d), F32),
                   jax.ShapeDtypeStruct((nbatch, n, d), F32),
                   jax.ShapeDtypeStruct((nbatch, n, qd), BF16)],
        compiler_params=_cparams("arbitrary", "arbitrary"),
        name="output_projection",
    )(hf, hb, o, cv_lat, x, mod, g_mhn, w_out, g_ffn, w_query)


def _top_rows(s_ref, pay_ref, val_ref, sel_ref, k):
    n = s_ref.shape[0]
    iota = lax.broadcasted_iota(jnp.int32, s_ref.shape, 0)

    def step(r, carry):
        s = s_ref[...]
        m = jnp.max(s, axis=0, keepdims=True)
        first = jnp.min(jnp.where(s == m, iota, n), axis=0, keepdims=True)
        hit = iota == first
        val_ref[pl.ds(r, 1), :] = m
        if pay_ref is None:
            sel_ref[pl.ds(r, 1), :] = first
        else:
            sel_ref[pl.ds(r, 1), :] = jnp.max(jnp.where(hit, pay_ref[...], -1), axis=0, keepdims=True)
        s_ref[...] = jnp.where(hit, NEG_INF, s)
        return carry

    lax.fori_loop(0, k, step, 0)


def _route_kernel(q_ref, keys_ref, e_ref, g_ref, s_ref, cand_ref, cidx_ref, va_ref, ia_ref, vb_ref, ib_ref,
                  best_ref, ex_ref, *, nheads, nkeys, half, topk):
    def head(h, carry):
        base = pl.multiple_of(h * 2 * half, 2 * half)
        s_ref[...] = _dot_nt(keys_ref[0], q_ref[:, pl.ds(base, half)])
        _top_rows(s_ref, None, va_ref, ia_ref, topk)
        s_ref[...] = _dot_nt(keys_ref[1], q_ref[:, pl.ds(base + half, half)])
        _top_rows(s_ref, None, vb_ref, ib_ref, topk)
        vb = vb_ref[...]
        ib = ib_ref[...]
        for a in range(topk):
            cand_ref[a * topk:(a + 1) * topk, :] = va_ref[a:a + 1, :] + vb
            cidx_ref[a * topk:(a + 1) * topk, :] = ia_ref[a:a + 1, :] * nkeys + ib
        _top_rows(cand_ref, cidx_ref, best_ref, ex_ref, topk)
        best = best_ref[...]
        ex = jnp.exp(best - best[0:1])
        rows = pl.ds(pl.multiple_of(h * topk, topk), topk)
        e_ref[rows, :] = ex_ref[...]
        g_ref[rows, :] = ex / jnp.sum(ex, axis=0, keepdims=True)
        return carry

    lax.fori_loop(0, nheads, head, 0)


def _peer_route(qp, keys, *, nheads, topk):
    t, qd = qp.shape
    nkeys, half = keys.shape[1], keys.shape[2]
    tb = 512 if t % 512 == 0 else ROW_BLOCK
    kern = functools.partial(_route_kernel, nheads=nheads, nkeys=nkeys, half=half, topk=topk)
    ne = nheads * topk
    return pl.pallas_call(
        kern,
        grid=(t // tb,),
        in_specs=[pl.BlockSpec((tb, qd), lambda i: (i, 0)), _resident(keys.shape)],
        out_specs=[pl.BlockSpec((ne, tb), lambda i: (0, i)), pl.BlockSpec((ne, tb), lambda i: (0, i))],
        out_shape=[jax.ShapeDtypeStruct((ne, t), jnp.int32), jax.ShapeDtypeStruct((ne, t), F32)],
        scratch_shapes=[pltpu.VMEM((nkeys, tb), F32),
                        pltpu.VMEM((topk * topk, tb), F32), pltpu.VMEM((topk * topk, tb), jnp.int32),
                        pltpu.VMEM((topk, tb), F32), pltpu.VMEM((topk, tb), jnp.int32),
                        pltpu.VMEM((topk, tb), F32), pltpu.VMEM((topk, tb), jnp.int32),
                        pltpu.VMEM((topk, tb), F32), pltpu.VMEM((topk, tb), jnp.int32)],
        compiler_params=_cparams("arbitrary"),
        name="peer_route",
    )(qp, keys)


def _pack_kernel(uv_ref, o_ref):
    o_ref[...] = pltpu.bitcast(uv_ref[...], jnp.uint32)


def _pack_tables(u_emb, v_emb):
    e, d = u_emb.shape
    rows = e * d // LANES
    rb = 8192
    uv = jnp.stack([u_emb.astype(BF16).reshape(rows, LANES), v_emb.astype(BF16).reshape(rows, LANES)],
                   axis=1).reshape(2 * rows, LANES)
    return pl.pallas_call(
        _pack_kernel,
        grid=(rows // rb,),
        in_specs=[pl.BlockSpec((2 * rb, LANES), lambda i: (i, 0))],
        out_specs=pl.BlockSpec((rb, LANES), lambda i: (i, 0)),
        out_shape=jax.ShapeDtypeStruct((rows, LANES), jnp.uint32),
        compiler_params=_cparams("arbitrary"),
        name="pack_tables",
    )(uv)


PEER_SLOTS = 8
PEER_AHEAD = 6


def _gelu(x):
    return 0.5 * x * (1.0 + lax.erf(x * 0.7071067811865476))


def _peer_kernel(ids_ref, nxt_ref, gate_ref, t_ref, x1_ref, mod_ref, gfin_ref, tab_ref, o_ref,
                 buf_ref, acc_ref, sem, *, d, n_per_batch):
    tb, ne = ids_ref.shape
    nchunk = d // LANES
    step = pl.program_id(0)
    b = (step * tb) // n_per_batch
    ones = jnp.ones((SUBLANES, LANES), BF16)

    def issue(src_ref, j, slot):
        for r in range(ne):
            row0 = pl.multiple_of(src_ref[j, r], nchunk)
            pltpu.make_async_copy(tab_ref.at[pl.ds(row0, nchunk), :],
                                  buf_ref.at[slot, pl.ds(r * nchunk, nchunk), :], sem.at[slot]).start()

    def wait(slot):
        pltpu.make_async_copy(tab_ref.at[pl.ds(0, ne * nchunk), :], buf_ref.at[slot], sem.at[slot]).wait()

    def words(slot, c):
        return buf_ref[slot, pl.ds(c, ne, stride=nchunk), :]

    def activations(j, slot):
        wait(slot)
        trow = t_ref[pl.ds(j, 1), :]
        part = jnp.zeros((ne, LANES), F32)
        for c in range(nchunk):
            u = lax.bitcast_convert_type(words(slot, c) << 16, F32)
            part = part + u * trow[:, c * LANES:(c + 1) * LANES]
        hi = part.astype(BF16)
        lo = (part - hi.astype(F32)).astype(BF16)
        score = _dot_nt(ones, hi) + _dot_nt(ones, lo)
        return _gelu(score) * gate_ref[pl.ds(j, 1), :]

    def mix(j, slot, act):
        a = act.astype(BF16)
        pieces = []
        for c in range(nchunk):
            v = lax.bitcast_convert_type(words(slot, c) & jnp.uint32(0xFFFF0000), F32).astype(BF16)
            pieces.append(_dot(a, v)[0:1])
        acc_ref[pl.ds(j, 1), :] = jnp.concatenate(pieces, axis=1)

    @pl.when(step == 0)
    def _():
        for j in range(PEER_AHEAD):
            issue(ids_ref, j, j)

    act0 = activations(0, 0)

    def main(j, act):
        issue(ids_ref, j + PEER_AHEAD - 1, (j + PEER_AHEAD - 1) % PEER_SLOTS)
        nxt = activations(j, j % PEER_SLOTS)
        mix(j - 1, (j - 1) % PEER_SLOTS, act)
        return nxt

    act = lax.fori_loop(1, tb - PEER_AHEAD + 1, main, act0)

    def tail(j, act):
        @pl.when(step + 1 < pl.num_programs(0))
        def _():
            issue(nxt_ref, j + PEER_AHEAD - 1 - tb, (j + PEER_AHEAD - 1) % PEER_SLOTS)

        nxt = activations(j, j % PEER_SLOTS)
        mix(j - 1, (j - 1) % PEER_SLOTS, act)
        return nxt

    act = lax.fori_loop(tb - PEER_AHEAD + 1, tb, tail, act)

    @pl.when(step + 1 < pl.num_programs(0))
    def _():
        issue(nxt_ref, PEER_AHEAD - 1, (tb + PEER_AHEAD - 1) % PEER_SLOTS)

    mix(tb - 1, (tb - 1) % PEER_SLOTS, act)

    x2 = x1_ref[...] + mod_ref[pl.ds(b, 1), 5 * d:6 * d] * acc_ref[...]
    o_ref[...] = _rms(x2, gfin_ref[...])


def _peer_mix(ids, gate, t2, x1, mod, g_final, table, *, n_per_batch):
    t, d = t2.shape
    ne = ids.shape[1]
    tb = PEER_TOKENS
    assert n_per_batch % tb == 0 and tb % PEER_SLOTS == 0 and PEER_AHEAD < PEER_SLOTS - 1
    nsteps = t // tb
    kern = functools.partial(_peer_kernel, d=d, n_per_batch=n_per_batch)
    row = lambda i: (i, 0)
    return pl.pallas_call(
        kern,
        grid=(nsteps,),
        in_specs=[pl.BlockSpec((tb, ne), row, memory_space=pltpu.SMEM),
                  pl.BlockSpec((tb, ne), lambda i: (jnp.minimum(i + 1, nsteps - 1), 0), memory_space=pltpu.SMEM),
                  pl.BlockSpec((tb, ne), row),
                  pl.BlockSpec((tb, d), row), pl.BlockSpec((tb, d), row),
                  _resident(mod.shape), _resident(g_final.shape),
                  pl.BlockSpec(memory_space=pl.ANY)],
        out_specs=pl.BlockSpec((tb, d), row),
        out_shape=jax.ShapeDtypeStruct((t, d), F32),
        scratch_shapes=[pltpu.VMEM((PEER_SLOTS, ne * d // LANES, LANES), jnp.uint32),
                        pltpu.VMEM((tb, d), F32),
                        pltpu.SemaphoreType.DMA((PEER_SLOTS,))],
        compiler_params=_cparams("arbitrary"),
        name="peer_mix",
    )(ids, ids, gate, t2, x1, mod, g_final, table)


def kernel(x, c, ctx, c_ctx, w_mod, b_mod, g_mix, g_ffn, w_in, b_gate, g_mhn, conv_w, conv_b,
           cn_g, cn_b, w_out, w_query, sub_keys, u_emb, v_emb, g_final):
    nbatch, n, d = x.shape
    ctx_len = ctx.shape[1]
    assert w_mod.shape[0] == 1, "single-layer kernel"
    mw = g_mhn.shape[1]
    cw = conv_b.shape[1]
    nh = M_HEADS
    hd = mw // nh
    ng = b_gate.shape[1]
    assert ng == 4 * nh and ng <= LANES

    cc = jnp.zeros((SUBLANES, d), F32).at[:nbatch].set(c).at[nbatch].set(c_ctx)
    mod = _modulation(cc, w_mod[0], b_mod)

    w = w_in[0]
    w_q, w_k, w_v, w_o = (w[:, j * mw:(j + 1) * mw] for j in range(4))
    w_g = w[:, 4 * mw:4 * mw + ng]
    w_cv = w[:, 4 * mw + ng:]
    w_main = jnp.concatenate([w_q * (hd ** -0.5), w_v, w_o, w_cv], axis=1).astype(BF16)
    w_kt = w_k.T.astype(BF16)
    w_gc = jnp.pad(w_g, ((0, 0), (0, LANES - ng))).astype(BF16)
    w_gt = w_g.T.astype(BF16)
    bias_row = jnp.pad(b_gate, ((0, 0), (0, LANES - ng)))
    bias_col = b_gate.reshape(ng, 1)

    qv, kt, o, cv, gcol, grow = _input_projection(x, ctx, mod, g_mix, w_main, w_kt, w_gc, w_gt,
                                                  bias_col, bias_row, mw=mw, cw=cw)
    hf = _mlstm_scan(qv, kt, gcol, grow, rev=False, nh=nh, hd=hd)
    hb = _mlstm_scan(qv, kt, gcol, grow, rev=True, nh=nh, hd=hd)
    cv_lat = _conformer_conv(cv, conv_w[0], conv_b, cn_g, cn_b, ctx_len=ctx_len, seq=GRID_W)
    x1, t2, qp = _output_projection(hf, hb, o, cv_lat, x, mod, g_mhn, w_out[0].astype(BF16), g_ffn,
                                    w_query[0].astype(BF16), ctx_len=ctx_len, nh=nh, hd=hd)

    t = nbatch * n
    eidx, gate = _peer_route(qp.reshape(t, -1), sub_keys[0].astype(BF16), nheads=PEER_HEADS, topk=PEER_TOPK)
    ids = eidx.T * (d // LANES)
    table = _pack_tables(u_emb[0], v_emb[0])
    out = _peer_mix(ids, gate.T, t2.reshape(t, d), x1.reshape(t, d), mod, g_final.reshape(1, d), table,
                    n_per_batch=n)
    return out.reshape(nbatch, n, d)
```

```python
import functools

import jax
import jax.numpy as jnp
from jax import lax
from jax.experimental import pallas as pl
from jax.experimental.pallas import tpu as pltpu

GRID_W = 64
EPS = 1e-6
M_HEADS = 4
PEER_HEADS = 8
PEER_TOPK = 16
N_MOD = 6

LANES = 128
SUBLANES = 8
M_CHUNK = 256
ROW_BLOCK = 256
PEER_TOKENS = 64
VMEM_LIMIT = 56 * 1024 * 1024

F32 = jnp.float32
BF16 = jnp.bfloat16
NEG_INF = float("-inf")


def _cparams(*sem):
    return pltpu.CompilerParams(dimension_semantics=sem, vmem_limit_bytes=VMEM_LIMIT)


def _resident(shape):
    nd = len(shape)
    return pl.BlockSpec(shape, lambda *_: (0,) * nd, pipeline_mode=pl.Buffered(1))


def _rms(x, g):
    return x * lax.rsqrt(jnp.mean(x * x, axis=-1, keepdims=True) + EPS) * g


def _dot(a, b):
    return jnp.dot(a, b, preferred_element_type=F32)


def _dot_nt(a, b):
    return lax.dot_general(a, b, (((1,), (1,)), ((), ())), preferred_element_type=F32)


def _mod_kernel(c_ref, w_ref, b_ref, o_ref):
    cc = c_ref[...]
    a = (cc * jax.nn.sigmoid(cc)).astype(BF16)
    o_ref[...] = _dot(a, w_ref[...].astype(BF16)) + b_ref[...]


def _modulation(cc, w_mod, b_mod):
    rows, d = cc.shape
    cols = w_mod.shape[1]
    tn = 1024
    return pl.pallas_call(
        _mod_kernel,
        grid=(cols // tn,),
        in_specs=[pl.BlockSpec((rows, d), lambda j: (0, 0)),
                  pl.BlockSpec((d, tn), lambda j: (0, j)),
                  pl.BlockSpec((1, tn), lambda j: (0, j))],
        out_specs=pl.BlockSpec((rows, tn), lambda j: (0, j)),
        out_shape=jax.ShapeDtypeStruct((rows, cols), F32),
        compiler_params=_cparams("arbitrary"),
        name="modulation",
    )(cc, w_mod, b_mod)


def _inproj_kernel(x_ref, ctx_ref, mod_ref, g_ref, w_ref, wkt_ref, wgc_ref, wgt_ref, bcol_ref, brow_ref,
                   qv_ref, kt_ref, o_ref, cv_ref, gcol_ref, grow_ref, *, nbatch, d, mw, cw):
    b = pl.program_id(0)
    i = pl.program_id(1)
    is_ctx = i == 0
    xin = jnp.where(is_ctx, ctx_ref[0], x_ref[0])
    r = jnp.where(is_ctx, nbatch, b)
    shift = mod_ref[pl.ds(r, 1), 0:d]
    scale = mod_ref[pl.ds(r, 1), d:2 * d]
    hx = (_rms(xin, g_ref[...]) * (1.0 + scale) + shift).astype(BF16)
    qv_ref[0] = _dot(hx, w_ref[:, 0:2 * mw]).astype(BF16)
    o_ref[0] = _dot(hx, w_ref[:, 2 * mw:3 * mw])
    cv_ref[0] = _dot(hx, w_ref[:, 3 * mw:3 * mw + 2 * cw])
    kt_ref[0] = _dot_nt(wkt_ref[...], hx).astype(BF16)
    gcol_ref[0] = _dot(hx, wgc_ref[...]) + brow_ref[...]
    grow_ref[0] = _dot_nt(wgt_ref[...], hx) + bcol_ref[...]


def _input_projection(x, ctx, mod, g_mix, w_main, w_kt, w_gc, w_gt, bias_col, bias_row, *, mw, cw):
    nbatch, n, d = x.shape
    ctx_len = ctx.shape[1]
    tm = ROW_BLOCK
    assert ctx_len == tm and n % tm == 0
    length = ctx_len + n
    nblk = length // tm
    ng = w_gt.shape[0]
    kern = functools.partial(_inproj_kernel, nbatch=nbatch, d=d, mw=mw, cw=cw)
    return pl.pallas_call(
        kern,
        grid=(nbatch, nblk),
        in_specs=[pl.BlockSpec((1, tm, d), lambda b, i: (b, jnp.maximum(i - 1, 0), 0)),
                  pl.BlockSpec((1, tm, d), lambda b, i: (b, 0, 0)),
                  _resident(mod.shape), _resident(g_mix.shape), _resident(w_main.shape),
                  _resident(w_kt.shape), _resident(w_gc.shape), _resident(w_gt.shape),
                  _resident(bias_col.shape), _resident(bias_row.shape)],
        out_specs=[pl.BlockSpec((1, tm, 2 * mw), lambda b, i: (b, i, 0)),
                   pl.BlockSpec((1, mw, tm), lambda b, i: (b, 0, i)),
                   pl.BlockSpec((1, tm, mw), lambda b, i: (b, i, 0)),
                   pl.BlockSpec((1, tm, 2 * cw), lambda b, i: (b, i, 0)),
                   pl.BlockSpec((1, tm, LANES), lambda b, i: (b, i, 0)),
                   pl.BlockSpec((1, ng, tm), lambda b, i: (b, 0, i))],
        out_shape=[jax.ShapeDtypeStruct((nbatch, length, 2 * mw), BF16),
                   jax.ShapeDtypeStruct((nbatch, mw, length), BF16),
                   jax.ShapeDtypeStruct((nbatch, length, mw), F32),
                   jax.ShapeDtypeStruct((nbatch, length, 2 * cw), F32),
                   jax.ShapeDtypeStruct((nbatch, length, LANES), F32),
                   jax.ShapeDtypeStruct((nbatch, ng, length), F32)],
        compiler_params=_cparams("arbitrary", "arbitrary"),
        name="input_projection",
    )(x, ctx, mod, g_mix, w_main, w_kt, w_gc, w_gt, bias_col, bias_row)


def _split3(a):
    a1 = a.astype(BF16)
    r1 = a - a1.astype(F32)
    a2 = r1.astype(BF16)
    a3 = (r1 - a2.astype(F32)).astype(BF16)
    return a1, a2, a3


def _log_sigmoid(g):
    return jnp.minimum(g, 0.0) - jnp.log1p(jnp.exp(-jnp.abs(g)))


def _mlstm_kernel(qv_ref, kt_ref, gcol_ref, grow_ref, h_ref, c_ref, m_ref, *, rev, nh, hd):
    ch = qv_ref.shape[1]

    @pl.when(pl.program_id(1) == 0)
    def _():
        c_ref[...] = jnp.zeros_like(c_ref)
        m_ref[...] = jnp.zeros_like(m_ref)

    row = lax.broadcasted_iota(jnp.int32, (ch, ch), 0)
    col = lax.broadcasted_iota(jnp.int32, (ch, ch), 1)
    allowed = (row <= col) if rev else (row >= col)
    allowed_t = (row >= col) if rev else (row <= col)
    tri = allowed.astype(BF16)
    tri_t = allowed_t.astype(BF16)

    gcol = gcol_ref[0]
    grow = grow_ref[0]
    bcol = sum(_dot(tri, p) for p in _split3(_log_sigmoid(gcol)))
    brow = sum(_dot(p, tri_t) for p in _split3(_log_sigmoid(grow)))
    last = 0 if rev else ch - 1
    ones = jnp.ones((ch, LANES), BF16)

    for h in range(nh):
        gi = (2 * nh if rev else 0) + h
        gf = gi + nh
        q = qv_ref[0, :, h * hd:(h + 1) * hd]
        v = qv_ref[0, :, (nh + h) * hd:(nh + h + 1) * hd]
        kt = kt_ref[0, h * hd:(h + 1) * hd, :]
        b_c = bcol[:, gf:gf + 1]
        b_r = brow[gf:gf + 1, :]
        i_c = gcol[:, gi:gi + 1]
        i_r = grow[gi:gi + 1, :]
        m = m_ref[h, 0:1, 0:1]

        log_d = jnp.where(allowed, b_c - b_r + i_r, NEG_INF)
        inter = b_c + m
        m_row = jnp.maximum(inter, jnp.max(log_d, axis=1, keepdims=True))
        w = jnp.exp(log_d - m_row)
        s_inter = jnp.exp(inter - m_row)
        qk = (_dot(q, kt) * w).astype(BF16)
        v_ext = jnp.concatenate([v, ones], axis=1)
        state = c_ref[h]
        nd = _dot(qk, v_ext) + s_inter * _dot(q, state.astype(BF16))
        den = nd[:, hd:hd + 1]
        h_ref[0, :, h * hd:(h + 1) * hd] = nd[:, 0:hd] / jnp.maximum(jnp.abs(den), jnp.exp(-m_row))

        b_last = brow[gf:gf + 1, last:last + 1]
        m_new = jnp.maximum(b_last + m, jnp.max(b_last - b_r + i_r, axis=1, keepdims=True))
        decay = jnp.exp(b_last + m - m_new)
        u_c = jnp.exp(b_last - b_c + i_c - m_new)
        uv = (u_c * v_ext.astype(F32)).astype(BF16)
        c_ref[h] = decay * state + _dot(kt, uv)
        m_ref[h] = jnp.broadcast_to(m_new, m_ref.shape[1:])


def _mlstm_scan(qv, kt, gcol, grow, *, rev, nh, hd):
    nbatch, length, _ = qv.shape
    ch = M_CHUNK
    nc = length // ch
    mw = nh * hd
    ng = grow.shape[1]
    if rev:
        blk = lambda c: jnp.where(c == 0, 0, nc - c)
    else:
        blk = lambda c: c
    kern = functools.partial(_mlstm_kernel, rev=rev, nh=nh, hd=hd)
    return pl.pallas_call(
        kern,
        grid=(nbatch, nc),
        in_specs=[pl.BlockSpec((1, ch, 2 * mw), lambda b, c: (b, blk(c), 0)),
                  pl.BlockSpec((1, mw, ch), lambda b, c: (b, 0, blk(c))),
                  pl.BlockSpec((1, ch, LANES), lambda b, c: (b, blk(c), 0)),
                  pl.BlockSpec((1, ng, ch), lambda b, c: (b, 0, blk(c)))],
        out_specs=pl.BlockSpec((1, ch, mw), lambda b, c: (b, blk(c), 0)),
        out_shape=jax.ShapeDtypeStruct((nbatch, length, mw), F32),
        scratch_shapes=[pltpu.VMEM((nh, hd, hd + LANES), F32),
                        pltpu.VMEM((nh, SUBLANES, LANES), F32)],
        compiler_params=_cparams("arbitrary", "arbitrary"),
        name="mlstm_bwd" if rev else "mlstm_fwd",
    )(qv, kt, gcol, grow)


def _conv_kernel(p_ref, w_ref, b_ref, g_ref, beta_ref, o_ref, pad_ref, y_ref, *, ks, seq, cw):
    tb = p_ref.shape[1]
    nseq = tb // seq
    half = ks // 2
    pad = 2 * SUBLANES
    assert half <= pad
    nchunk = cw // LANES

    u = p_ref[0, :, 0:cw] * jax.nn.sigmoid(p_ref[0, :, cw:2 * cw])
    zeros = jnp.zeros((pad, cw), F32)
    for s in range(nseq):
        pad_ref[s, 0:pad, :] = zeros
        pad_ref[s, pad:pad + seq, :] = u[s * seq:(s + 1) * seq]
        pad_ref[s, pad + seq:pad + seq + pad, :] = zeros

    def body(idx, carry):
        s = idx // nchunk
        c0 = pl.multiple_of((idx % nchunk) * LANES, LANES)
        acc = jnp.zeros((seq, LANES), F32)
        for k in range(ks):
            acc = acc + pad_ref[s, pl.ds(pad - half + k, seq), pl.ds(c0, LANES)] * w_ref[k:k + 1, pl.ds(c0, LANES)]
        y_ref[pl.ds(pl.multiple_of(s * seq, seq), seq), pl.ds(c0, LANES)] = acc + b_ref[:, pl.ds(c0, LANES)]
        return carry

    lax.fori_loop(0, nseq * nchunk, body, 0)

    y = y_ref[...]
    mu = jnp.mean(y, axis=-1, keepdims=True)
    dev = y - mu
    z = dev * lax.rsqrt(jnp.mean(dev * dev, axis=-1, keepdims=True) + EPS) * g_ref[...] + beta_ref[...]
    o_ref[0] = (z * jax.nn.sigmoid(z)).astype(BF16)


def _conformer_conv(cv, conv_w, conv_b, ln_g, ln_b, *, ctx_len, seq):
    nbatch, length, cw2 = cv.shape
    cw = cw2 // 2
    n = length - ctx_len
    tb = ROW_BLOCK
    assert ctx_len % tb == 0 and n % tb == 0 and tb % seq == 0
    off = ctx_len // tb
    ks = conv_w.shape[0]
    kern = functools.partial(_conv_kernel, ks=ks, seq=seq, cw=cw)
    return pl.pallas_call(
        kern,
        grid=(nbatch, n // tb),
        in_specs=[pl.BlockSpec((1, tb, cw2), lambda b, i: (b, i + off, 0)),
                  _resident(conv_w.shape), _resident(conv_b.shape), _resident(ln_g.shape), _resident(ln_b.shape)],
        out_specs=pl.BlockSpec((1, tb, cw), lambda b, i: (b, i, 0)),
        out_shape=jax.ShapeDtypeStruct((nbatch, n, cw), BF16),
        scratch_shapes=[pltpu.VMEM((tb // seq, seq + 4 * SUBLANES, cw), F32),
                        pltpu.VMEM((tb, cw), F32)],
        compiler_params=_cparams("arbitrary", "arbitrary"),
        name="conformer_conv",
    )(cv, conv_w, conv_b, ln_g, ln_b)


def _outproj_kernel(hf_ref, hb_ref, o_ref, cv_ref, x_ref, mod_ref, gm_ref, wo_ref, gf_ref, wq_ref,
                    x1_ref, t2_ref, qp_ref, *, d, nh, hd):
    b = pl.program_id(0)
    hsum = hf_ref[0] + hb_ref[0]
    parts = []
    for h in range(nh):
        seg = hsum[:, h * hd:(h + 1) * hd]
        parts.append(seg * lax.rsqrt(jnp.mean(seg * seg, axis=-1, keepdims=True) + EPS))
    hn = jnp.concatenate(parts, axis=1)
    m_lat = (hn * gm_ref[...] * jax.nn.sigmoid(o_ref[0])).astype(BF16)
    lhs = jnp.concatenate([m_lat, cv_ref[0]], axis=1)
    gate1 = mod_ref[pl.ds(b, 1), 2 * d:3 * d]
    x1 = x_ref[0] + gate1 * _dot(lhs, wo_ref[...])
    x1_ref[0] = x1
    shift2 = mod_ref[pl.ds(b, 1), 3 * d:4 * d]
    scale2 = mod_ref[pl.ds(b, 1), 4 * d:5 * d]
    t2 = _rms(x1, gf_ref[...]) * (1.0 + scale2) + shift2
    t2_ref[0] = t2
    qp_ref[0] = _dot(t2.astype(BF16), wq_ref[...]).astype(BF16)


def _output_projection(hf, hb, o, cv_lat, x, mod, g_mhn, w_out, g_ffn, w_query, *, ctx_len, nh, hd):
    nbatch, n, d = x.shape
    mw = nh * hd
    cw = cv_lat.shape[2]
    tm = ROW_BLOCK
    off = ctx_len // tm
    qd = w_query.shape[1]
    kern = functools.partial(_outproj_kernel, d=d, nh=nh, hd=hd)
    lat = lambda b, i: (b, i + off, 0)
    cur = lambda b, i: (b, i, 0)
    return pl.pallas_call(
        kern,
        grid=(nbatch, n // tm),
        in_specs=[pl.BlockSpec((1, tm, mw), lat), pl.BlockSpec((1, tm, mw), lat), pl.BlockSpec((1, tm, mw), lat),
                  pl.BlockSpec((1, tm, cw), cur), pl.BlockSpec((1, tm, d), cur),
                  _resident(mod.shape), _resident(g_mhn.shape), _resident(w_out.shape),
                  _resident(g_ffn.shape), _resident(w_query.shape)],
        out_specs=[pl.BlockSpec((1, tm, d), cur), pl.BlockSpec((1, tm, d), cur), pl.BlockSpec((1, tm, qd), cur)],
        out_shape=[jax.ShapeDtypeStruct((nbatch, n, d), F32),
                   jax.ShapeDtypeStruct((nbatch, n, d), F32),
                   jax.ShapeDtypeStruct((nbatch, n, qd), BF16)],
        compiler_params=_cparams("arbitrary", "arbitrary"),
        name="output_projection",
    )(hf, hb, o, cv_lat, x, mod, g_mhn, w_out, g_ffn, w_query)


def _top_rows(s_ref, pay_ref, val_ref, sel_ref, k):
    n = s_ref.shape[0]
    iota = lax.broadcasted_iota(jnp.int32, s_ref.shape, 0)

    def step(r, carry):
        s = s_ref[...]
        m = jnp.max(s, axis=0, keepdims=True)
        first = jnp.min(jnp.where(s == m, iota, n), axis=0, keepdims=True)
        hit = iota == first
        val_ref[pl.ds(r, 1), :] = m
        if pay_ref is None:
            sel_ref[pl.ds(r, 1), :] = first
        else:
            sel_ref[pl.ds(r, 1), :] = jnp.max(jnp.where(hit, pay_ref[...], -1), axis=0, keepdims=True)
        s_ref[...] = jnp.where(hit, NEG_INF, s)
        return carry

    lax.fori_loop(0, k, step, 0)


def _route_kernel(q_ref, keys_ref, e_ref, g_ref, s_ref, cand_ref, cidx_ref, va_ref, ia_ref, vb_ref, ib_ref,
                  best_ref, ex_ref, *, nheads, nkeys, half, topk):
    def head(h, carry):
        base = pl.multiple_of(h * 2 * half, 2 * half)
        s_ref[...] = _dot_nt(keys_ref[0], q_ref[:, pl.ds(base, half)])
        _top_rows(s_ref, None, va_ref, ia_ref, topk)
        s_ref[...] = _dot_nt(keys_ref[1], q_ref[:, pl.ds(base + half, half)])
        _top_rows(s_ref, None, vb_ref, ib_ref, topk)
        row = 0
        for a in range(topk):
            nb = topk // (a + 1)
            cand_ref[row:row + nb, :] = va_ref[a:a + 1, :] + vb_ref[0:nb, :]
            cidx_ref[row:row + nb, :] = ia_ref[a:a + 1, :] * nkeys + ib_ref[0:nb, :]
            row += nb
        pad = cand_ref.shape[0] - row
        if pad:
            cand_ref[row:, :] = jnp.full((pad, cand_ref.shape[1]), NEG_INF, F32)
            cidx_ref[row:, :] = jnp.zeros((pad, cand_ref.shape[1]), jnp.int32)
        _top_rows(cand_ref, cidx_ref, best_ref, ex_ref, topk)
        best = best_ref[...]
        ex = jnp.exp(best - best[0:1])
        rows = pl.ds(pl.multiple_of(h * topk, topk), topk)
        e_ref[rows, :] = ex_ref[...]
        g_ref[rows, :] = ex / jnp.sum(ex, axis=0, keepdims=True)
        return carry

    lax.fori_loop(0, nheads, head, 0)


def _peer_route(qp, keys, *, nheads, topk):
    t, qd = qp.shape
    nkeys, half = keys.shape[1], keys.shape[2]
    tb = 512 if t % 512 == 0 else ROW_BLOCK
    kern = functools.partial(_route_kernel, nheads=nheads, nkeys=nkeys, half=half, topk=topk)
    ne = nheads * topk
    ncand = sum(topk // (a + 1) for a in range(topk))
    ncand = -(-ncand // SUBLANES) * SUBLANES
    return pl.pallas_call(
        kern,
        grid=(t // tb,),
        in_specs=[pl.BlockSpec((tb, qd), lambda i: (i, 0)), _resident(keys.shape)],
        out_specs=[pl.BlockSpec((ne, tb), lambda i: (0, i)), pl.BlockSpec((ne, tb), lambda i: (0, i))],
        out_shape=[jax.ShapeDtypeStruct((ne, t), jnp.int32), jax.ShapeDtypeStruct((ne, t), F32)],
        scratch_shapes=[pltpu.VMEM((nkeys, tb), F32),
                        pltpu.VMEM((ncand, tb), F32), pltpu.VMEM((ncand, tb), jnp.int32),
                        pltpu.VMEM((topk, tb), F32), pltpu.VMEM((topk, tb), jnp.int32),
                        pltpu.VMEM((topk, tb), F32), pltpu.VMEM((topk, tb), jnp.int32),
                        pltpu.VMEM((topk, tb), F32), pltpu.VMEM((topk, tb), jnp.int32)],
        compiler_params=_cparams("arbitrary"),
        name="peer_route",
    )(qp, keys)


HI_HALF = 0xFFFF0000


def _pack_kernel(u_ref, v_ref, o_ref):
    eb, d = u_ref.shape
    nchunk = d // LANES
    for c in range(nchunk):
        lanes = slice(c * LANES, (c + 1) * LANES)
        ub = lax.bitcast_convert_type(u_ref[:, lanes].astype(BF16).astype(F32), jnp.uint32)
        vb = lax.bitcast_convert_type(v_ref[:, lanes].astype(BF16).astype(F32), jnp.uint32)
        o_ref[pl.ds(c, eb, stride=nchunk), :] = (ub >> 16) | (vb & jnp.uint32(HI_HALF))


def _pack_tables(u_emb, v_emb):
    e, d = u_emb.shape
    nchunk = d // LANES
    eb = 256
    return pl.pallas_call(
        _pack_kernel,
        grid=(e // eb,),
        in_specs=[pl.BlockSpec((eb, d), lambda i: (i, 0)), pl.BlockSpec((eb, d), lambda i: (i, 0))],
        out_specs=pl.BlockSpec((eb * nchunk, LANES), lambda i: (i, 0)),
        out_shape=jax.ShapeDtypeStruct((e * nchunk, LANES), jnp.uint32),
        compiler_params=_cparams("arbitrary"),
        name="pack_tables",
    )(u_emb, v_emb)


PEER_SLOTS = 8
PEER_AHEAD = 6


def _gelu(x):
    return 0.5 * x * (1.0 + lax.erf(x * 0.7071067811865476))


def _peer_kernel(ids_ref, nxt_ref, gate_ref, t_ref, x1_ref, mod_ref, gfin_ref, tab_ref, o_ref, *scratch,
                 d, n_per_batch):
    bufs = scratch[:PEER_SLOTS]
    stage = scratch[PEER_SLOTS:PEER_SLOTS + 4]
    part_s = scratch[PEER_SLOTS + 4:PEER_SLOTS + 6]
    act_s = scratch[PEER_SLOTS + 6:PEER_SLOTS + 8]
    acc_ref, sem = scratch[PEER_SLOTS + 8:]
    tb, ne = ids_ref.shape
    nchunk = d // LANES
    step = pl.program_id(0)
    has_next = step + 1 < pl.num_programs(0)
    b = (step * tb) // n_per_batch
    ones = jnp.ones((SUBLANES, LANES), BF16)

    def issue(src_ref, j, k, part=0, nparts=1):
        per = ne // nparts
        for r in range(part * per, (part + 1) * per):
            row0 = pl.multiple_of(src_ref[j, r], nchunk)
            pltpu.make_async_copy(tab_ref.at[pl.ds(row0, nchunk), :],
                                  bufs[k].at[pl.ds(r * nchunk, nchunk), :], sem.at[k]).start()

    def wait(k):
        pltpu.make_async_copy(tab_ref.at[pl.ds(0, ne * nchunk), :], bufs[k], sem.at[k]).wait()

    def unpack(j, k, prefetch):
        wait(k)
        vst = stage[k % 4]
        trow = t_ref[pl.ds(j, 1), :]
        part = jnp.zeros((ne, LANES), F32)
        for c in range(nchunk):
            lanes = slice(c * LANES, (c + 1) * LANES)
            w = bufs[k][pl.ds(c, ne, stride=nchunk), :]
            part = part + lax.bitcast_convert_type(w << 16, F32) * trow[:, lanes]
            vst[:, lanes] = lax.bitcast_convert_type(w & jnp.uint32(HI_HALF), F32).astype(BF16)
            prefetch(c, nchunk)
        part_s[k % 2][...] = part

    def activate(row, k):
        part = part_s[k % 2][...]
        hi = part.astype(BF16)
        lo = (part - hi.astype(F32)).astype(BF16)
        score = _dot_nt(ones, hi) + _dot_nt(ones, lo)
        act_s[k % 2][...] = _gelu(score) * gate_ref[pl.ds(row, 1), :]

    def mix(row, k):
        acc_ref[pl.ds(row, 1), :] = _dot(act_s[k % 2][...].astype(BF16), stage[k % 4][...])[0:1]

    def region(j, k, prefetch):
        unpack(j, k, prefetch)
        activate(jnp.maximum(j - 1, 0), (k - 1) % PEER_SLOTS)
        mix(jnp.maximum(j - 2, 0), (k - 2) % PEER_SLOTS)

    @pl.when(step == 0)
    def _():
        for j in range(PEER_AHEAD):
            issue(ids_ref, j, j)
        for ref in (*stage, *part_s, *act_s):
            ref[...] = jnp.zeros_like(ref)

    def group(o, carry):
        for k in range(PEER_SLOTS):
            j = o * PEER_SLOTS + k
            region(j, k, lambda p, n, j=j, k=k: issue(ids_ref, j + PEER_AHEAD, (k + PEER_AHEAD) % PEER_SLOTS, p, n))
        return carry

    ngroups = tb // PEER_SLOTS
    lax.fori_loop(0, ngroups - 1, group, 0)

    for k in range(PEER_SLOTS):
        j = tb - PEER_SLOTS + k
        ahead = j + PEER_AHEAD
        slot = (k + PEER_AHEAD) % PEER_SLOTS
        if ahead < tb:
            region(j, k, lambda p, n, ahead=ahead, slot=slot: issue(ids_ref, ahead, slot, p, n))
        else:
            region(j, k, lambda p, n, ahead=ahead, slot=slot: p == 0 and pl.when(has_next)(
                lambda: issue(nxt_ref, ahead - tb, slot)))

    activate(tb - 1, PEER_SLOTS - 1)
    mix(tb - 2, PEER_SLOTS - 2)
    mix(tb - 1, PEER_SLOTS - 1)

    x2 = x1_ref[...] + mod_ref[pl.ds(b, 1), 5 * d:6 * d] * acc_ref[...]
    o_ref[...] = _rms(x2, gfin_ref[...])


def _peer_mix(ids, gate, t2, x1, mod, g_final, table, *, n_per_batch):
    t, d = t2.shape
    ne = ids.shape[1]
    tb = PEER_TOKENS
    assert n_per_batch % tb == 0 and tb % PEER_SLOTS == 0 and tb >= 2 * PEER_SLOTS and PEER_AHEAD < PEER_SLOTS
    nsteps = t // tb
    kern = functools.partial(_peer_kernel, d=d, n_per_batch=n_per_batch)
    row = lambda i: (i, 0)
    return pl.pallas_call(
        kern,
        grid=(nsteps,),
        in_specs=[pl.BlockSpec((tb, ne), row, memory_space=pltpu.SMEM),
                  pl.BlockSpec((tb, ne), lambda i: (jnp.minimum(i + 1, nsteps - 1), 0), memory_space=pltpu.SMEM),
                  pl.BlockSpec((tb, ne), row),
                  pl.BlockSpec((tb, d), row), pl.BlockSpec((tb, d), row),
                  _resident(mod.shape), _resident(g_final.shape),
                  pl.BlockSpec(memory_space=pl.ANY)],
        out_specs=pl.BlockSpec((tb, d), row),
        out_shape=jax.ShapeDtypeStruct((t, d), F32),
        scratch_shapes=([pltpu.VMEM((ne * d // LANES, LANES), jnp.uint32) for _ in range(PEER_SLOTS)]
                        + [pltpu.VMEM((ne, d), BF16) for _ in range(4)]
                        + [pltpu.VMEM((ne, LANES), F32) for _ in range(2)]
                        + [pltpu.VMEM((SUBLANES, LANES), F32) for _ in range(2)]
                        + [pltpu.VMEM((tb, d), F32), pltpu.SemaphoreType.DMA((PEER_SLOTS,))]),
        compiler_params=_cparams("arbitrary"),
        name="peer_mix",
    )(ids, ids, gate, t2, x1, mod, g_final, table)


def kernel(x, c, ctx, c_ctx, w_mod, b_mod, g_mix, g_ffn, w_in, b_gate, g_mhn, conv_w, conv_b,
           cn_g, cn_b, w_out, w_query, sub_keys, u_emb, v_emb, g_final):
    nbatch, n, d = x.shape
    ctx_len = ctx.shape[1]
    assert w_mod.shape[0] == 1, "single-layer kernel"
    mw = g_mhn.shape[1]
    cw = conv_b.shape[1]
    nh = M_HEADS
    hd = mw // nh
    ng = b_gate.shape[1]
    assert ng == 4 * nh and ng <= LANES

    cc = jnp.zeros((SUBLANES, d), F32).at[:nbatch].set(c).at[nbatch].set(c_ctx)
    mod = _modulation(cc, w_mod[0], b_mod)

    w = w_in[0]
    w_q, w_k, w_v, w_o = (w[:, j * mw:(j + 1) * mw] for j in range(4))
    w_g = w[:, 4 * mw:4 * mw + ng]
    w_cv = w[:, 4 * mw + ng:]
    w_main = jnp.concatenate([w_q * (hd ** -0.5), w_v, w_o, w_cv], axis=1).astype(BF16)
    w_kt = w_k.T.astype(BF16)
    w_gc = jnp.pad(w_g, ((0, 0), (0, LANES - ng))).astype(BF16)
    w_gt = w_g.T.astype(BF16)
    bias_row = jnp.pad(b_gate, ((0, 0), (0, LANES - ng)))
    bias_col = b_gate.reshape(ng, 1)

    qv, kt, o, cv, gcol, grow = _input_projection(x, ctx, mod, g_mix, w_main, w_kt, w_gc, w_gt,
                                                  bias_col, bias_row, mw=mw, cw=cw)
    hf = _mlstm_scan(qv, kt, gcol, grow, rev=False, nh=nh, hd=hd)
    hb = _mlstm_scan(qv, kt, gcol, grow, rev=True, nh=nh, hd=hd)
    cv_lat = _conformer_conv(cv, conv_w[0], conv_b, cn_g, cn_b, ctx_len=ctx_len, seq=GRID_W)
    x1, t2, qp = _output_projection(hf, hb, o, cv_lat, x, mod, g_mhn, w_out[0].astype(BF16), g_ffn,
                                    w_query[0].astype(BF16), ctx_len=ctx_len, nh=nh, hd=hd)

    t = nbatch * n
    eidx, gate = _peer_route(qp.reshape(t, -1), sub_keys[0].astype(BF16), nheads=PEER_HEADS, topk=PEER_TOPK)
    ids = eidx.T * (d // LANES)
    table = _pack_tables(u_emb[0], v_emb[0])
    out = _peer_mix(ids, gate.T, t2.reshape(t, d), x1.reshape(t, d), mod, g_final.reshape(1, d), table,
                    n_per_batch=n)
    return out.reshape(nbatch, n, d)
```

```python
import functools

import jax
import jax.numpy as jnp
from jax import lax
from jax.experimental import pallas as pl
from jax.experimental.pallas import tpu as pltpu

GRID_W = 64
EPS = 1e-6
M_HEADS = 4
PEER_HEADS = 8
PEER_TOPK = 16
N_MOD = 6

LANES = 128
SUBLANES = 8
M_CHUNK = 256
ROW_BLOCK = 256
PEER_TOKENS = 64
VMEM_LIMIT = 56 * 1024 * 1024

F32 = jnp.float32
BF16 = jnp.bfloat16
NEG_INF = float("-inf")


def _cparams(*sem):
    return pltpu.CompilerParams(dimension_semantics=sem, vmem_limit_bytes=VMEM_LIMIT)


def _resident(shape):
    nd = len(shape)
    return pl.BlockSpec(shape, lambda *_: (0,) * nd, pipeline_mode=pl.Buffered(1))


def _rms(x, g):
    return x * lax.rsqrt(jnp.mean(x * x, axis=-1, keepdims=True) + EPS) * g


def _dot(a, b):
    return jnp.dot(a, b, preferred_element_type=F32)


def _dot_nt(a, b):
    return lax.dot_general(a, b, (((1,), (1,)), ((), ())), preferred_element_type=F32)


def _mod_kernel(c_ref, w_ref, b_ref, o_ref):
    cc = c_ref[...]
    a = (cc * jax.nn.sigmoid(cc)).astype(BF16)
    o_ref[...] = _dot(a, w_ref[...].astype(BF16)) + b_ref[...]


def _modulation(cc, w_mod, b_mod):
    rows, d = cc.shape
    cols = w_mod.shape[1]
    tn = 1024
    return pl.pallas_call(
        _mod_kernel,
        grid=(cols // tn,),
        in_specs=[pl.BlockSpec((rows, d), lambda j: (0, 0)),
                  pl.BlockSpec((d, tn), lambda j: (0, j)),
                  pl.BlockSpec((1, tn), lambda j: (0, j))],
        out_specs=pl.BlockSpec((rows, tn), lambda j: (0, j)),
        out_shape=jax.ShapeDtypeStruct((rows, cols), F32),
        compiler_params=_cparams("arbitrary"),
        name="modulation",
    )(cc, w_mod, b_mod)


def _inproj_kernel(x_ref, ctx_ref, mod_ref, g_ref, w_ref, wkt_ref, wgc_ref, wgt_ref, bcol_ref, brow_ref,
                   qv_ref, kt_ref, o_ref, cv_ref, gcol_ref, grow_ref, *, nbatch, d, mw, cw):
    b = pl.program_id(0)
    i = pl.program_id(1)
    is_ctx = i == 0
    xin = jnp.where(is_ctx, ctx_ref[0], x_ref[0])
    r = jnp.where(is_ctx, nbatch, b)
    shift = mod_ref[pl.ds(r, 1), 0:d]
    scale = mod_ref[pl.ds(r, 1), d:2 * d]
    hx = (_rms(xin, g_ref[...]) * (1.0 + scale) + shift).astype(BF16)
    qv_ref[0] = _dot(hx, w_ref[:, 0:2 * mw]).astype(BF16)
    o_ref[0] = _dot(hx, w_ref[:, 2 * mw:3 * mw])
    cv_ref[0] = _dot(hx, w_ref[:, 3 * mw:3 * mw + 2 * cw])
    kt_ref[0] = _dot_nt(wkt_ref[...], hx).astype(BF16)
    gcol_ref[0] = _dot(hx, wgc_ref[...]) + brow_ref[...]
    grow_ref[0] = _dot_nt(wgt_ref[...], hx) + bcol_ref[...]


def _input_projection(x, ctx, mod, g_mix, w_main, w_kt, w_gc, w_gt, bias_col, bias_row, *, mw, cw):
    nbatch, n, d = x.shape
    ctx_len = ctx.shape[1]
    tm = ROW_BLOCK
    assert ctx_len == tm and n % tm == 0
    length = ctx_len + n
    nblk = length // tm
    ng = w_gt.shape[0]
    kern = functools.partial(_inproj_kernel, nbatch=nbatch, d=d, mw=mw, cw=cw)
    return pl.pallas_call(
        kern,
        grid=(nbatch, nblk),
        in_specs=[pl.BlockSpec((1, tm, d), lambda b, i: (b, jnp.maximum(i - 1, 0), 0)),
                  pl.BlockSpec((1, tm, d), lambda b, i: (b, 0, 0)),
                  _resident(mod.shape), _resident(g_mix.shape), _resident(w_main.shape),
                  _resident(w_kt.shape), _resident(w_gc.shape), _resident(w_gt.shape),
                  _resident(bias_col.shape), _resident(bias_row.shape)],
        out_specs=[pl.BlockSpec((1, tm, 2 * mw), lambda b, i: (b, i, 0)),
                   pl.BlockSpec((1, mw, tm), lambda b, i: (b, 0, i)),
                   pl.BlockSpec((1, tm, mw), lambda b, i: (b, i, 0)),
                   pl.BlockSpec((1, tm, 2 * cw), lambda b, i: (b, i, 0)),
                   pl.BlockSpec((1, tm, LANES), lambda b, i: (b, i, 0)),
                   pl.BlockSpec((1, ng, tm), lambda b, i: (b, 0, i))],
        out_shape=[jax.ShapeDtypeStruct((nbatch, length, 2 * mw), BF16),
                   jax.ShapeDtypeStruct((nbatch, mw, length), BF16),
                   jax.ShapeDtypeStruct((nbatch, length, mw), F32),
                   jax.ShapeDtypeStruct((nbatch, length, 2 * cw), F32),
                   jax.ShapeDtypeStruct((nbatch, length, LANES), F32),
                   jax.ShapeDtypeStruct((nbatch, ng, length), F32)],
        compiler_params=_cparams("arbitrary", "arbitrary"),
        name="input_projection",
    )(x, ctx, mod, g_mix, w_main, w_kt, w_gc, w_gt, bias_col, bias_row)


def _split3(a):
    a1 = a.astype(BF16)
    r1 = a - a1.astype(F32)
    a2 = r1.astype(BF16)
    a3 = (r1 - a2.astype(F32)).astype(BF16)
    return a1, a2, a3


def _log_sigmoid(g):
    return jnp.minimum(g, 0.0) - jnp.log1p(jnp.exp(-jnp.abs(g)))


def _mlstm_kernel(qv_ref, kt_ref, gcol_ref, grow_ref, h_ref, c_ref, m_ref, *, rev, nh, hd):
    ch = qv_ref.shape[1]

    @pl.when(pl.program_id(1) == 0)
    def _():
        c_ref[...] = jnp.zeros_like(c_ref)
        m_ref[...] = jnp.zeros_like(m_ref)

    row = lax.broadcasted_iota(jnp.int32, (ch, ch), 0)
    col = lax.broadcasted_iota(jnp.int32, (ch, ch), 1)
    allowed = (row <= col) if rev else (row >= col)
    allowed_t = (row >= col) if rev else (row <= col)
    tri = allowed.astype(BF16)
    tri_t = allowed_t.astype(BF16)

    gcol = gcol_ref[0]
    grow = grow_ref[0]
    bcol = sum(_dot(tri, p) for p in _split3(_log_sigmoid(gcol)))
    brow = sum(_dot(p, tri_t) for p in _split3(_log_sigmoid(grow)))
    last = 0 if rev else ch - 1
    ones = jnp.ones((ch, LANES), BF16)

    for h in range(nh):
        gi = (2 * nh if rev else 0) + h
        gf = gi + nh
        q = qv_ref[0, :, h * hd:(h + 1) * hd]
        v = qv_ref[0, :, (nh + h) * hd:(nh + h + 1) * hd]
        kt = kt_ref[0, h * hd:(h + 1) * hd, :]
        b_c = bcol[:, gf:gf + 1]
        b_r = brow[gf:gf + 1, :]
        i_c = gcol[:, gi:gi + 1]
        i_r = grow[gi:gi + 1, :]
        m = m_ref[h, 0:1, 0:1]

        log_d = jnp.where(allowed, b_c - b_r + i_r, NEG_INF)
        inter = b_c + m
        m_row = jnp.maximum(inter, jnp.max(log_d, axis=1, keepdims=True))
        w = jnp.exp(log_d - m_row)
        s_inter = jnp.exp(inter - m_row)
        qk = (_dot(q, kt) * w).astype(BF16)
        v_ext = jnp.concatenate([v, ones], axis=1)
        state = c_ref[h]
        nd = _dot(qk, v_ext) + s_inter * _dot(q, state.astype(BF16))
        den = nd[:, hd:hd + 1]
        h_ref[0, :, h * hd:(h + 1) * hd] = nd[:, 0:hd] / jnp.maximum(jnp.abs(den), jnp.exp(-m_row))

        b_last = brow[gf:gf + 1, last:last + 1]
        m_new = jnp.maximum(b_last + m, jnp.max(b_last - b_r + i_r, axis=1, keepdims=True))
        decay = jnp.exp(b_last + m - m_new)
        u_c = jnp.exp(b_last - b_c + i_c - m_new)
        uv = (u_c * v_ext.astype(F32)).astype(BF16)
        c_ref[h] = decay * state + _dot(kt, uv)
        m_ref[h] = jnp.broadcast_to(m_new, m_ref.shape[1:])


def _mlstm_scan(qv, kt, gcol, grow, *, rev, nh, hd):
    nbatch, length, _ = qv.shape
    ch = M_CHUNK
    nc = length // ch
    mw = nh * hd
    ng = grow.shape[1]
    if rev:
        blk = lambda c: jnp.where(c == 0, 0, nc - c)
    else:
        blk = lambda c: c
    kern = functools.partial(_mlstm_kernel, rev=rev, nh=nh, hd=hd)
    return pl.pallas_call(
        kern,
        grid=(nbatch, nc),
        in_specs=[pl.BlockSpec((1, ch, 2 * mw), lambda b, c: (b, blk(c), 0)),
                  pl.BlockSpec((1, mw, ch), lambda b, c: (b, 0, blk(c))),
                  pl.BlockSpec((1, ch, LANES), lambda b, c: (b, blk(c), 0)),
                  pl.BlockSpec((1, ng, ch), lambda b, c: (b, 0, blk(c)))],
        out_specs=pl.BlockSpec((1, ch, mw), lambda b, c: (b, blk(c), 0)),
        out_shape=jax.ShapeDtypeStruct((nbatch, length, mw), F32),
        scratch_shapes=[pltpu.VMEM((nh, hd, hd + LANES), F32),
                        pltpu.VMEM((nh, SUBLANES, LANES), F32)],
        compiler_params=_cparams("arbitrary", "arbitrary"),
        name="mlstm_bwd" if rev else "mlstm_fwd",
    )(qv, kt, gcol, grow)


def _conv_kernel(p_ref, w_ref, b_ref, g_ref, beta_ref, o_ref, pad_ref, y_ref, *, ks, seq, cw):
    tb = p_ref.shape[1]
    nseq = tb // seq
    half = ks // 2
    pad = 2 * SUBLANES
    assert half <= pad
    nchunk = cw // LANES

    u = p_ref[0, :, 0:cw] * jax.nn.sigmoid(p_ref[0, :, cw:2 * cw])
    zeros = jnp.zeros((pad, cw), F32)
    for s in range(nseq):
        pad_ref[s, 0:pad, :] = zeros
        pad_ref[s, pad:pad + seq, :] = u[s * seq:(s + 1) * seq]
        pad_ref[s, pad + seq:pad + seq + pad, :] = zeros

    def body(idx, carry):
        s = idx // nchunk
        c0 = pl.multiple_of((idx % nchunk) * LANES, LANES)
        acc = jnp.zeros((seq, LANES), F32)
        for k in range(ks):
            acc = acc + pad_ref[s, pl.ds(pad - half + k, seq), pl.ds(c0, LANES)] * w_ref[k:k + 1, pl.ds(c0, LANES)]
        y_ref[pl.ds(pl.multiple_of(s * seq, seq), seq), pl.ds(c0, LANES)] = acc + b_ref[:, pl.ds(c0, LANES)]
        return carry

    lax.fori_loop(0, nseq * nchunk, body, 0)

    y = y_ref[...]
    mu = jnp.mean(y, axis=-1, keepdims=True)
    dev = y - mu
    z = dev * lax.rsqrt(jnp.mean(dev * dev, axis=-1, keepdims=True) + EPS) * g_ref[...] + beta_ref[...]
    o_ref[0] = (z * jax.nn.sigmoid(z)).astype(BF16)


def _conformer_conv(cv, conv_w, conv_b, ln_g, ln_b, *, ctx_len, seq):
    nbatch, length, cw2 = cv.shape
    cw = cw2 // 2
    n = length - ctx_len
    tb = ROW_BLOCK
    assert ctx_len % tb == 0 and n % tb == 0 and tb % seq == 0
    off = ctx_len // tb
    ks = conv_w.shape[0]
    kern = functools.partial(_conv_kernel, ks=ks, seq=seq, cw=cw)
    return pl.pallas_call(
        kern,
        grid=(nbatch, n // tb),
        in_specs=[pl.BlockSpec((1, tb, cw2), lambda b, i: (b, i + off, 0)),
                  _resident(conv_w.shape), _resident(conv_b.shape), _resident(ln_g.shape), _resident(ln_b.shape)],
        out_specs=pl.BlockSpec((1, tb, cw), lambda b, i: (b, i, 0)),
        out_shape=jax.ShapeDtypeStruct((nbatch, n, cw), BF16),
        scratch_shapes=[pltpu.VMEM((tb // seq, seq + 4 * SUBLANES, cw), F32),
                        pltpu.VMEM((tb, cw), F32)],
        compiler_params=_cparams("arbitrary", "arbitrary"),
        name="conformer_conv",
    )(cv, conv_w, conv_b, ln_g, ln_b)


def _outproj_kernel(hf_ref, hb_ref, o_ref, cv_ref, x_ref, mod_ref, gm_ref, wo_ref, gf_ref, wq_ref,
                    x1_ref, t2_ref, qp_ref, *, d, nh, hd):
    b = pl.program_id(0)
    hsum = hf_ref[0] + hb_ref[0]
    parts = []
    for h in range(nh):
        seg = hsum[:, h * hd:(h + 1) * hd]
        parts.append(seg * lax.rsqrt(jnp.mean(seg * seg, axis=-1, keepdims=True) + EPS))
    hn = jnp.concatenate(parts, axis=1)
    m_lat = (hn * gm_ref[...] * jax.nn.sigmoid(o_ref[0])).astype(BF16)
    lhs = jnp.concatenate([m_lat, cv_ref[0]], axis=1)
    gate1 = mod_ref[pl.ds(b, 1), 2 * d:3 * d]
    x1 = x_ref[0] + gate1 * _dot(lhs, wo_ref[...])
    x1_ref[0] = x1
    shift2 = mod_ref[pl.ds(b, 1), 3 * d:4 * d]
    scale2 = mod_ref[pl.ds(b, 1), 4 * d:5 * d]
    t2 = _rms(x1, gf_ref[...]) * (1.0 + scale2) + shift2
    t2_ref[0] = t2
    qp_ref[0] = _dot(t2.astype(BF16), wq_ref[...]).astype(BF16)


def _output_projection(hf, hb, o, cv_lat, x, mod, g_mhn, w_out, g_ffn, w_query, *, ctx_len, nh, hd):
    nbatch, n, d = x.shape
    mw = nh * hd
    cw = cv_lat.shape[2]
    tm = ROW_BLOCK
    off = ctx_len // tm
    qd = w_query.shape[1]
    kern = functools.partial(_outproj_kernel, d=d, nh=nh, hd=hd)
    lat = lambda b, i: (b, i + off, 0)
    cur = lambda b, i: (b, i, 0)
    return pl.pallas_call(
        kern,
        grid=(nbatch, n // tm),
        in_specs=[pl.BlockSpec((1, tm, mw), lat), pl.BlockSpec((1, tm, mw), lat), pl.BlockSpec((1, tm, mw), lat),
                  pl.BlockSpec((1, tm, cw), cur), pl.BlockSpec((1, tm, d), cur),
                  _resident(mod.shape), _resident(g_mhn.shape), _resident(w_out.shape),
                  _resident(g_ffn.shape), _resident(w_query.shape)],
        out_specs=[pl.BlockSpec((1, tm, d), cur), pl.BlockSpec((1, tm, d), cur), pl.BlockSpec((1, tm, qd), cur)],
        out_shape=[jax.ShapeDtypeStruct((nbatch, n, d), F32),
                   jax.ShapeDtypeStruct((nbatch, n, d), F32),
                   jax.ShapeDtypeStruct((nbatch, n, qd), BF16)],
        compiler_params=_cparams("arbitrary", "arbitrary"),
        name="output_projection",
    )(hf, hb, o, cv_lat, x, mod, g_mhn, w_out, g_ffn, w_query)


def _top_rows(s_ref, pay_ref, val_ref, sel_ref, k):
    n = s_ref.shape[0]
    iota = lax.broadcasted_iota(jnp.int32, s_ref.shape, 0)

    def step(r, carry):
        s = s_ref[...]
        m = jnp.max(s, axis=0, keepdims=True)
        first = jnp.min(jnp.where(s == m, iota, n), axis=0, keepdims=True)
        hit = iota == first
        val_ref[pl.ds(r, 1), :] = m
        if pay_ref is None:
            sel_ref[pl.ds(r, 1), :] = first
        else:
            sel_ref[pl.ds(r, 1), :] = jnp.max(jnp.where(hit, pay_ref[...], -1), axis=0, keepdims=True)
        s_ref[...] = jnp.where(hit, NEG_INF, s)
        return carry

    lax.fori_loop(0, k, step, 0)


def _route_kernel(q_ref, keys_ref, e_ref, g_ref, s_ref, cand_ref, cidx_ref, va_ref, ia_ref, vb_ref, ib_ref,
                  best_ref, ex_ref, *, nheads, nkeys, half, topk):
    def head(h, carry):
        base = pl.multiple_of(h * 2 * half, 2 * half)
        s_ref[...] = _dot_nt(keys_ref[0], q_ref[:, pl.ds(base, half)])
        _top_rows(s_ref, None, va_ref, ia_ref, topk)
        s_ref[...] = _dot_nt(keys_ref[1], q_ref[:, pl.ds(base + half, half)])
        _top_rows(s_ref, None, vb_ref, ib_ref, topk)
        row = 0
        for a in range(topk):
            nb = topk // (a + 1)
            cand_ref[row:row + nb, :] = va_ref[a:a + 1, :] + vb_ref[0:nb, :]
            cidx_ref[row:row + nb, :] = ia_ref[a:a + 1, :] * nkeys + ib_ref[0:nb, :]
            row += nb
        pad = cand_ref.shape[0] - row
        if pad:
            cand_ref[row:, :] = jnp.full((pad, cand_ref.shape[1]), NEG_INF, F32)
            cidx_ref[row:, :] = jnp.zeros((pad, cand_ref.shape[1]), jnp.int32)
        _top_rows(cand_ref, cidx_ref, best_ref, ex_ref, topk)
        best = best_ref[...]
        ex = jnp.exp(best - best[0:1])
        rows = pl.ds(pl.multiple_of(h * topk, topk), topk)
        e_ref[rows, :] = ex_ref[...]
        g_ref[rows, :] = ex / jnp.sum(ex, axis=0, keepdims=True)
        return carry

    lax.fori_loop(0, nheads, head, 0)


def _peer_route(qp, keys, *, nheads, topk):
    t, qd = qp.shape
    nkeys, half = keys.shape[1], keys.shape[2]
    tb = 512 if t % 512 == 0 else ROW_BLOCK
    kern = functools.partial(_route_kernel, nheads=nheads, nkeys=nkeys, half=half, topk=topk)
    ne = nheads * topk
    ncand = sum(topk // (a + 1) for a in range(topk))
    ncand = -(-ncand // SUBLANES) * SUBLANES
    return pl.pallas_call(
        kern,
        grid=(t // tb,),
        in_specs=[pl.BlockSpec((tb, qd), lambda i: (i, 0)), _resident(keys.shape)],
        out_specs=[pl.BlockSpec((ne, tb), lambda i: (0, i)), pl.BlockSpec((ne, tb), lambda i: (0, i))],
        out_shape=[jax.ShapeDtypeStruct((ne, t), jnp.int32), jax.ShapeDtypeStruct((ne, t), F32)],
        scratch_shapes=[pltpu.VMEM((nkeys, tb), F32),
                        pltpu.VMEM((ncand, tb), F32), pltpu.VMEM((ncand, tb), jnp.int32),
                        pltpu.VMEM((topk, tb), F32), pltpu.VMEM((topk, tb), jnp.int32),
                        pltpu.VMEM((topk, tb), F32), pltpu.VMEM((topk, tb), jnp.int32),
                        pltpu.VMEM((topk, tb), F32), pltpu.VMEM((topk, tb), jnp.int32)],
        compiler_params=_cparams("arbitrary"),
        name="peer_route",
    )(qp, keys)


HI_HALF = 0xFFFF0000


def _pack_kernel(u_ref, v_ref, o_ref):
    eb, d = u_ref.shape
    nchunk = d // LANES
    for c in range(nchunk):
        lanes = slice(c * LANES, (c + 1) * LANES)
        ub = lax.bitcast_convert_type(u_ref[:, lanes].astype(BF16).astype(F32), jnp.uint32)
        vb = lax.bitcast_convert_type(v_ref[:, lanes].astype(BF16).astype(F32), jnp.uint32)
        o_ref[pl.ds(c, eb, stride=nchunk), :] = (ub >> 16) | (vb & jnp.uint32(HI_HALF))


def _pack_tables(u_emb, v_emb):
    e, d = u_emb.shape
    nchunk = d // LANES
    eb = 256
    return pl.pallas_call(
        _pack_kernel,
        grid=(e // eb,),
        in_specs=[pl.BlockSpec((eb, d), lambda i: (i, 0)), pl.BlockSpec((eb, d), lambda i: (i, 0))],
        out_specs=pl.BlockSpec((eb * nchunk, LANES), lambda i: (i, 0)),
        out_shape=jax.ShapeDtypeStruct((e * nchunk, LANES), jnp.uint32),
        compiler_params=_cparams("arbitrary"),
        name="pack_tables",
    )(u_emb, v_emb)


PEER_SLOTS = 8
PEER_AHEAD = 6


def _gelu(x):
    return 0.5 * x * (1.0 + lax.erf(x * 0.7071067811865476))


def _peer_kernel(ids_ref, nxt_ref, gate_ref, t_ref, x1_ref, mod_ref, gfin_ref, tab_ref, o_ref, *scratch,
                 d, n_per_batch):
    bufs = scratch[:PEER_SLOTS]
    stage = scratch[PEER_SLOTS:PEER_SLOTS + 4]
    part_s = scratch[PEER_SLOTS + 4:PEER_SLOTS + 6]
    act_s = scratch[PEER_SLOTS + 6:PEER_SLOTS + 8]
    acc_ref, sem = scratch[PEER_SLOTS + 8:]
    tb, ne = ids_ref.shape
    nchunk = d // LANES
    step = pl.program_id(0)
    has_next = step + 1 < pl.num_programs(0)
    b = (step * tb) // n_per_batch
    ones = jnp.ones((SUBLANES, LANES), BF16)

    def issue(src_ref, j, k, part=0, nparts=1):
        per = ne // nparts
        for r in range(part * per, (part + 1) * per):
            row0 = pl.multiple_of(src_ref[j, r], nchunk)
            pltpu.make_async_copy(tab_ref.at[pl.ds(row0, nchunk), :],
                                  bufs[k].at[pl.ds(r * nchunk, nchunk), :], sem.at[k]).start(priority=r % 2)

    def wait(k):
        pltpu.make_async_copy(tab_ref.at[pl.ds(0, ne * nchunk), :], bufs[k], sem.at[k]).wait()

    def unpack(j, k, prefetch):
        wait(k)
        vst = stage[k % 4]
        trow = t_ref[pl.ds(j, 1), :]
        part = jnp.zeros((ne, LANES), F32)
        for c in range(nchunk):
            lanes = slice(c * LANES, (c + 1) * LANES)
            w = bufs[k][pl.ds(c, ne, stride=nchunk), :]
            part = part + lax.bitcast_convert_type(w << 16, F32) * trow[:, lanes]
            vst[:, lanes] = lax.bitcast_convert_type(w & jnp.uint32(HI_HALF), F32).astype(BF16)
        part_s[k % 2][...] = part

    def activate(row, k):
        part = part_s[k % 2][...]
        hi = part.astype(BF16)
        lo = (part - hi.astype(F32)).astype(BF16)
        score = _dot_nt(ones, hi) + _dot_nt(ones, lo)
        act_s[k % 2][...] = _gelu(score) * gate_ref[pl.ds(row, 1), :]

    def mix(row, k):
        acc_ref[pl.ds(row, 1), :] = _dot(act_s[k % 2][...].astype(BF16), stage[k % 4][...])[0:1]

    def region(j, k, prefetch):
        unpack(j, k, prefetch)
        activate(jnp.maximum(j - 1, 0), (k - 1) % PEER_SLOTS)
        mix(jnp.maximum(j - 2, 0), (k - 2) % PEER_SLOTS)
        prefetch(0, 1)

    @pl.when(step == 0)
    def _():
        for j in range(PEER_AHEAD):
            issue(ids_ref, j, j)
        for ref in (*stage, *part_s, *act_s):
            ref[...] = jnp.zeros_like(ref)

    def group(o, carry):
        for k in range(PEER_SLOTS):
            j = o * PEER_SLOTS + k
            region(j, k, lambda p, n, j=j, k=k: issue(ids_ref, j + PEER_AHEAD, (k + PEER_AHEAD) % PEER_SLOTS, p, n))
        return carry

    ngroups = tb // PEER_SLOTS
    lax.fori_loop(0, ngroups - 1, group, 0)

    for k in range(PEER_SLOTS):
        j = tb - PEER_SLOTS + k
        ahead = j + PEER_AHEAD
        slot = (k + PEER_AHEAD) % PEER_SLOTS
        if ahead < tb:
            region(j, k, lambda p, n, ahead=ahead, slot=slot: issue(ids_ref, ahead, slot, p, n))
        else:
            region(j, k, lambda p, n, ahead=ahead, slot=slot: p == 0 and pl.when(has_next)(
                lambda: issue(nxt_ref, ahead - tb, slot)))

    activate(tb - 1, PEER_SLOTS - 1)
    mix(tb - 2, PEER_SLOTS - 2)
    mix(tb - 1, PEER_SLOTS - 1)

    x2 = x1_ref[...] + mod_ref[pl.ds(b, 1), 5 * d:6 * d] * acc_ref[...]
    o_ref[...] = _rms(x2, gfin_ref[...])


def _peer_mix(ids, gate, t2, x1, mod, g_final, table, *, n_per_batch):
    t, d = t2.shape
    ne = ids.shape[1]
    tb = PEER_TOKENS
    assert n_per_batch % tb == 0 and tb % PEER_SLOTS == 0 and tb >= 2 * PEER_SLOTS and PEER_AHEAD < PEER_SLOTS
    nsteps = t // tb
    kern = functools.partial(_peer_kernel, d=d, n_per_batch=n_per_batch)
    row = lambda i: (i, 0)
    return pl.pallas_call(
        kern,
        grid=(nsteps,),
        in_specs=[pl.BlockSpec((tb, ne), row, memory_space=pltpu.SMEM),
                  pl.BlockSpec((tb, ne), lambda i: (jnp.minimum(i + 1, nsteps - 1), 0), memory_space=pltpu.SMEM),
                  pl.BlockSpec((tb, ne), row),
                  pl.BlockSpec((tb, d), row), pl.BlockSpec((tb, d), row),
                  _resident(mod.shape), _resident(g_final.shape),
                  pl.BlockSpec(memory_space=pl.ANY)],
        out_specs=pl.BlockSpec((tb, d), row),
        out_shape=jax.ShapeDtypeStruct((t, d), F32),
        scratch_shapes=([pltpu.VMEM((ne * d // LANES, LANES), jnp.uint32) for _ in range(PEER_SLOTS)]
                        + [pltpu.VMEM((ne, d), BF16) for _ in range(4)]
                        + [pltpu.VMEM((ne, LANES), F32) for _ in range(2)]
                        + [pltpu.VMEM((SUBLANES, LANES), F32) for _ in range(2)]
                        + [pltpu.VMEM((tb, d), F32), pltpu.SemaphoreType.DMA((PEER_SLOTS,))]),
        compiler_params=_cparams("arbitrary"),
        name="peer_mix",
    )(ids, ids, gate, t2, x1, mod, g_final, table)


def kernel(x, c, ctx, c_ctx, w_mod, b_mod, g_mix, g_ffn, w_in, b_gate, g_mhn, conv_w, conv_b,
           cn_g, cn_b, w_out, w_query, sub_keys, u_emb, v_emb, g_final):
    nbatch, n, d = x.shape
    ctx_len = ctx.shape[1]
    assert w_mod.shape[0] == 1, "single-layer kernel"
    mw = g_mhn.shape[1]
    cw = conv_b.shape[1]
    nh = M_HEADS
    hd = mw // nh
    ng = b_gate.shape[1]
    assert ng == 4 * nh and ng <= LANES

    cc = jnp.zeros((SUBLANES, d), F32).at[:nbatch].set(c).at[nbatch].set(c_ctx)
    mod = _modulation(cc, w_mod[0], b_mod)

    w = w_in[0]
    w_q, w_k, w_v, w_o = (w[:, j * mw:(j + 1) * mw] for j in range(4))
    w_g = w[:, 4 * mw:4 * mw + ng]
    w_cv = w[:, 4 * mw + ng:]
    w_main = jnp.concatenate([w_q * (hd ** -0.5), w_v, w_o, w_cv], axis=1).astype(BF16)
    w_kt = w_k.T.astype(BF16)
    w_gc = jnp.pad(w_g, ((0, 0), (0, LANES - ng))).astype(BF16)
    w_gt = w_g.T.astype(BF16)
    bias_row = jnp.pad(b_gate, ((0, 0), (0, LANES - ng)))
    bias_col = b_gate.reshape(ng, 1)

    qv, kt, o, cv, gcol, grow = _input_projection(x, ctx, mod, g_mix, w_main, w_kt, w_gc, w_gt,
                                                  bias_col, bias_row, mw=mw, cw=cw)
    hf = _mlstm_scan(qv, kt, gcol, grow, rev=False, nh=nh, hd=hd)
    hb = _mlstm_scan(qv, kt, gcol, grow, rev=True, nh=nh, hd=hd)
    cv_lat = _conformer_conv(cv, conv_w[0], conv_b, cn_g, cn_b, ctx_len=ctx_len, seq=GRID_W)
    x1, t2, qp = _output_projection(hf, hb, o, cv_lat, x, mod, g_mhn, w_out[0].astype(BF16), g_ffn,
                                    w_query[0].astype(BF16), ctx_len=ctx_len, nh=nh, hd=hd)

    t = nbatch * n
    eidx, gate = _peer_route(qp.reshape(t, -1), sub_keys[0].astype(BF16), nheads=PEER_HEADS, topk=PEER_TOPK)
    ids = eidx.T * (d // LANES)
    table = _pack_tables(u_emb[0], v_emb[0])
    out = _peer_mix(ids, gate.T, t2.reshape(t, d), x1.reshape(t, d), mod, g_final.reshape(1, d), table,
                    n_per_batch=n)
    return out.reshape(nbatch, n, d)
```
